```python
import jax, jax.numpy as jnp
from jax import lax
import numpy as np

D_MODEL = 1024
BATCH = 4
SEQ = 8192
DEPTH = 1

SB_HEADS = 8
SB_HEAD_DIM = 64
SB_WIDTH = SB_HEADS * SB_HEAD_DIM
BLOCK_Q = 128
CONV_GROUPS = 8
CONV_WIDTH = D_MODEL // 2
CONV_KERNEL = 3
N_BRANCHES = 2
SPLIT_SIZES = (SB_WIDTH, SB_WIDTH, SB_WIDTH, SB_WIDTH,
               CONV_WIDTH, CONV_WIDTH, CONV_WIDTH, CONV_WIDTH,
               D_MODEL, D_MODEL)
IN_COLS = 4 * SB_WIDTH + 4 * CONV_WIDTH + N_BRANCHES * D_MODEL
DEEPNORM_ALPHA = (2.0 * DEPTH) ** 0.25
DEEPNORM_BETA = (8.0 * DEPTH) ** -0.25
LN_EPS = 1e-5

kernel_name = "hybrid_stickbreak_shortconv_gated_deepnorm"


def _split_points():
    pts, acc = [], 0
    for sz in SPLIT_SIZES[:-1]:
        acc += sz
        pts.append(acc)
    return pts


def layer_norm(h, gain, bias):
    h32 = h.astype(jnp.float32)
    mu = jnp.mean(h32, axis=-1, keepdims=True)
    var = jnp.mean(jnp.square(h32 - mu), axis=-1, keepdims=True)
    out = (h32 - mu) * lax.rsqrt(var + LN_EPS) * gain.astype(jnp.float32) + bias.astype(jnp.float32)
    return out.astype(h.dtype)


def stick_breaking_attention(q, k, v):
    b, h, s, dh = q.shape
    nb = s // BLOCK_Q
    scale = dh ** -0.5
    qb = q.reshape(b, h, nb, BLOCK_Q, dh).transpose(2, 0, 1, 3, 4)
    idx = jnp.arange(BLOCK_Q)
    strict = idx[:, None] > idx[None, :]

    def one_query_block(args):
        i, q_i = args
        q32 = q_i.astype(jnp.float32) * scale

        def body(step, carry):
            acc, log_surv = carry
            j = i - step
            k_j = lax.dynamic_slice_in_dim(k, j * BLOCK_Q, BLOCK_Q, axis=2).astype(jnp.float32)
            v_j = lax.dynamic_slice_in_dim(v, j * BLOCK_Q, BLOCK_Q, axis=2).astype(jnp.float32)
            z = jnp.einsum('bhqd,bhkd->bhqk', q32, k_j)
            mask = jnp.where(j == i, strict, True)
            log_fail = jnp.where(mask, jax.nn.log_sigmoid(-z), 0.0)
            after = lax.cumsum(log_fail, axis=3, reverse=True) - log_fail
            log_a = jax.nn.log_sigmoid(z) + after + log_surv[..., None]
            a = jnp.where(mask, jnp.exp(log_a), 0.0)
            acc = acc + jnp.einsum('bhqk,bhkd->bhqd', a, v_j)
            log_surv = log_surv + jnp.sum(log_fail, axis=3)
            return acc, log_surv

        init = (jnp.zeros((b, h, BLOCK_Q, dh), jnp.float32),
                jnp.zeros((b, h, BLOCK_Q), jnp.float32))
        acc, _ = lax.fori_loop(0, i + 1, body, init)
        return acc

    out = lax.map(one_query_block, (jnp.arange(nb), qb))
    return out.transpose(1, 2, 0, 3, 4).reshape(b, h, s, dh).astype(q.dtype)


def short_causal_conv(h, conv_w):
    s = h.shape[1]
    hp = jnp.pad(h, ((0, 0), (CONV_KERNEL - 1, 0), (0, 0)))
    out = conv_w[CONV_KERNEL - 1] * hp[:, CONV_KERNEL - 1:CONV_KERNEL - 1 + s]
    for tap in range(CONV_KERNEL - 1):
        out = out + conv_w[tap] * hp[:, tap:tap + s]
    return out


def setup_inputs(seed: int = 0) -> dict:
    key = jax.random.key(seed)
    ks = jax.random.split(key, 8)
    x = jax.random.normal(ks[0], (BATCH, SEQ, D_MODEL), jnp.float32)
    col_scale = jnp.concatenate([
        jnp.ones((2 * SB_WIDTH,), jnp.float32),
        jnp.full((SB_WIDTH,), DEEPNORM_BETA, jnp.float32),
        jnp.ones((SB_WIDTH,), jnp.float32),
        jnp.full((CONV_WIDTH,), DEEPNORM_BETA, jnp.float32),
        jnp.ones((3 * CONV_WIDTH,), jnp.float32),
        jnp.ones((N_BRANCHES * D_MODEL,), jnp.float32),
    ])
    w_in = jax.random.normal(ks[1], (DEPTH, D_MODEL, IN_COLS), jnp.float32) * (D_MODEL ** -0.5) * col_scale
    conv_w = jax.random.normal(ks[2], (DEPTH, CONV_KERNEL, CONV_WIDTH), jnp.float32) * (CONV_KERNEL ** -0.5)
    w_proj_attn = jax.random.normal(ks[3], (DEPTH, SB_WIDTH, D_MODEL), jnp.float32) * (SB_WIDTH ** -0.5) * DEEPNORM_BETA
    w_proj_conv = jax.random.normal(ks[4], (DEPTH, CONV_WIDTH, D_MODEL), jnp.float32) * (CONV_WIDTH ** -0.5) * DEEPNORM_BETA
    w_out = jax.random.normal(ks[5], (DEPTH, D_MODEL, D_MODEL), jnp.float32) * (D_MODEL ** -0.5) * DEEPNORM_BETA
    ln_gain = 1.0 + 0.01 * jax.random.normal(ks[6], (DEPTH, D_MODEL), jnp.float32)
    ln_bias = 0.01 * jax.random.normal(ks[7], (DEPTH, D_MODEL), jnp.float32)
    return {"x": x, "w_in": w_in, "conv_w": conv_w, "w_proj_attn": w_proj_attn,
            "w_proj_conv": w_proj_conv, "w_out": w_out, "ln_gain": ln_gain, "ln_bias": ln_bias}


def reference(x, w_in, conv_w, w_proj_attn, w_proj_conv, w_out, ln_gain, ln_bias):
    h = x
    pts = _split_points()
    for layer in range(DEPTH):
        b, s, _ = h.shape
        proj = jnp.einsum('bsd,dc->bsc', h, w_in[layer])
        q, k, v, z_attn, u, gate_b, gate_c, z_conv, g_attn, g_conv = jnp.split(proj, pts, axis=-1)

        def to_heads(t):
            return t.reshape(b, s, SB_HEADS, SB_HEAD_DIM).transpose(0, 2, 1, 3)
        o_attn = stick_breaking_attention(to_heads(q), to_heads(k), to_heads(v))
        o_attn = o_attn.transpose(0, 2, 1, 3).reshape(b, s, SB_WIDTH) * jax.nn.silu(z_attn)

        o_conv = gate_b * short_causal_conv(gate_c * u, conv_w[layer]) * jax.nn.silu(z_conv)

        y_attn = jnp.einsum('bsc,cd->bsd', o_attn, w_proj_attn[layer])
        y_conv = jnp.einsum('bsc,cd->bsd', o_conv, w_proj_conv[layer])
        merged = jax.nn.sigmoid(g_attn) * y_attn + jax.nn.sigmoid(g_conv) * y_conv
        sub = jnp.einsum('bsd,de->bse', merged, w_out[layer])

        h = layer_norm(DEEPNORM_ALPHA * h + sub, ln_gain[layer], ln_bias[layer])
    return h
```

```python
import functools

import jax
import jax.numpy as jnp
from jax import lax
from jax.experimental import pallas as pl
from jax.experimental.pallas import tpu as pltpu

D_MODEL = 1024
SB_HEADS = 8
SB_HEAD_DIM = 64
SB_WIDTH = SB_HEADS * SB_HEAD_DIM
CONV_WIDTH = D_MODEL // 2
CONV_KERNEL = 3
DEPTH = 1
DEEPNORM_ALPHA = (2.0 * DEPTH) ** 0.25
LN_EPS = 1e-5

LANES = 128
SUBLANES = 8
HEAD_PAIRS = SB_WIDTH // LANES
BLOCK = 128
ROW_TILE = 512
VMEM_LIMIT_BYTES = 56 * 1024 * 1024

BF16 = jnp.bfloat16
F32 = jnp.float32


def _in_proj_kernel(x_ref, w_ref, q_ref, k_ref, v_ref, za_ref, u_ref, gb_ref, gc_ref, zc_ref,
                    ga_ref, gv_ref):
    xb = x_ref[0].astype(BF16)

    def seg(c0, width):
        return jnp.dot(xb, w_ref[:, c0:c0 + width], preferred_element_type=F32).astype(BF16)

    for idx, ref in enumerate((q_ref, k_ref, v_ref)):
        y = seg(idx * SB_WIDTH, SB_WIDTH)
        for p in range(HEAD_PAIRS):
            ref[0, p] = y[:, p * LANES:(p + 1) * LANES]
    c0 = 3 * SB_WIDTH
    for ref in (za_ref, u_ref, gb_ref, gc_ref, zc_ref):
        ref[0] = seg(c0, SB_WIDTH)
        c0 += SB_WIDTH
    for ref in (ga_ref, gv_ref):
        ref[0] = seg(c0, D_MODEL)
        c0 += D_MODEL


def _in_proj(x, w_in_bf16):
    b, s, d = x.shape
    tm = ROW_TILE
    grid = (b, s // tm)
    pair_shape = jax.ShapeDtypeStruct((b, HEAD_PAIRS, s, LANES), BF16)
    half_shape = jax.ShapeDtypeStruct((b, s, SB_WIDTH), BF16)
    full_shape = jax.ShapeDtypeStruct((b, s, D_MODEL), BF16)
    pair_spec = pl.BlockSpec((1, HEAD_PAIRS, tm, LANES), lambda bi, ti: (bi, 0, ti, 0))
    half_spec = pl.BlockSpec((1, tm, SB_WIDTH), lambda bi, ti: (bi, ti, 0))
    full_spec = pl.BlockSpec((1, tm, D_MODEL), lambda bi, ti: (bi, ti, 0))
    return pl.pallas_call(
        _in_proj_kernel,
        grid=grid,
        in_specs=[
            pl.BlockSpec((1, tm, d), lambda bi, ti: (bi, ti, 0)),
            pl.BlockSpec(w_in_bf16.shape, lambda bi, ti: (0, 0), pipeline_mode=pl.Buffered(1)),
        ],
        out_specs=[pair_spec] * 3 + [half_spec] * 5 + [full_spec] * 2,
        out_shape=[pair_shape] * 3 + [half_shape] * 5 + [full_shape] * 2,
        compiler_params=pltpu.CompilerParams(
            dimension_semantics=("parallel", "parallel"), vmem_limit_bytes=VMEM_LIMIT_BYTES),
        name="in_proj",
    )(x, w_in_bf16)


def _attn_kernel(q_ref, k_ref, v_ref, uu_ref, o_ref, qs_ref, acc_ref, ls_ref):
    s_len = q_ref.shape[2]
    nb = s_len // BLOCK
    lane = lax.broadcasted_iota(jnp.int32, (BLOCK, LANES), 1)
    first_head = lane < SB_HEAD_DIM
    row2 = lax.broadcasted_iota(jnp.int32, (2 * BLOCK, BLOCK), 0) & (BLOCK - 1)
    col2 = lax.broadcasted_iota(jnp.int32, (2 * BLOCK, BLOCK), 1)
    strict = col2 < row2

    def tile(j, masked):
        start = pl.multiple_of(j * BLOCK, BLOCK)
        k2 = k_ref[0, 0, pl.ds(start, BLOCK), :]
        v2 = v_ref[0, 0, pl.ds(start, BLOCK), :]
        z = lax.dot_general(qs_ref[...], k2, (((1,), (1,)), ((), ())),
                            preferred_element_type=F32)
        softplus = jnp.maximum(z, 0.0) + jnp.log(1.0 + jnp.exp(-jnp.abs(z)))
        log_fail = -softplus
        log_hit = z - softplus
        if masked:
            log_fail = jnp.where(strict, log_fail, 0.0)
        hi = log_fail.astype(BF16)
        lo = (log_fail - hi.astype(F32)).astype(BF16)
        sums = jnp.dot(jnp.concatenate([hi, lo], axis=1), uu_ref[...],
                       preferred_element_type=F32)
        after = sums[:, :BLOCK]
        total = sums[:, BLOCK:]
        log_surv = ls_ref[...]
        a = jnp.exp(log_hit + after + log_surv)
        if masked:
            a = jnp.where(strict, a, 0.0)
        acc_ref[...] += jnp.dot(a.astype(BF16), v2, preferred_element_type=F32)
        ls_ref[...] = log_surv + total

    def query_block(i, carry):
        start = pl.multiple_of(i * BLOCK, BLOCK)
        q2 = q_ref[0, 0, pl.ds(start, BLOCK), :] * jnp.asarray(SB_HEAD_DIM ** -0.5, BF16)
        zero = jnp.zeros_like(q2)
        qs_ref[...] = jnp.concatenate(
            [jnp.where(first_head, q2, zero), jnp.where(first_head, zero, q2)], axis=0)
        acc_ref[...] = jnp.zeros_like(acc_ref)
        ls_ref[...] = jnp.zeros_like(ls_ref)
        tile(i, True)

        def off_diagonal(step, c):
            tile(i - 1 - step, False)
            return c

        lax.fori_loop(0, i, off_diagonal, 0)
        acc = acc_ref[...]
        o_ref[0, 0, pl.ds(start, BLOCK), :] = jnp.where(
            first_head, acc[:BLOCK], acc[BLOCK:]).astype(o_ref.dtype)
        return carry

    lax.fori_loop(0, nb, query_block, 0)


def _cumsum_weights():
    key = jnp.arange(2 * BLOCK)[:, None] % BLOCK
    col = jnp.arange(2 * BLOCK)[None, :]
    return jnp.where((col >= BLOCK) | (key > col), 1.0, 0.0).astype(BF16)


def _attention(q4, k4, v4):
    b, pairs, s, _ = q4.shape
    seq_spec = pl.BlockSpec((1, 1, s, LANES), lambda bi, pi: (bi, pi, 0, 0))
    uu = _cumsum_weights()
    return pl.pallas_call(
        _attn_kernel,
        grid=(b, pairs),
        in_specs=[seq_spec, seq_spec, seq_spec,
                  pl.BlockSpec(uu.shape, lambda bi, pi: (0, 0))],
        out_specs=seq_spec,
        out_shape=jax.ShapeDtypeStruct(q4.shape, BF16),
        scratch_shapes=[
            pltpu.VMEM((2 * BLOCK, LANES), BF16),
            pltpu.VMEM((2 * BLOCK, LANES), F32),
            pltpu.VMEM((2 * BLOCK, BLOCK), F32),
        ],
        compiler_params=pltpu.CompilerParams(
            dimension_semantics=("parallel", "parallel"), vmem_limit_bytes=VMEM_LIMIT_BYTES),
        name="stickbreak_attn",
    )(q4, k4, v4, uu)


def _sigmoid(x):
    return 1.0 / (1.0 + jnp.exp(-x))


def _shift_rows(cur, prev_tail, shift):
    rolled = pltpu.roll(cur, shift, 0)
    tail = pltpu.roll(prev_tail, shift, 0)
    row = lax.broadcasted_iota(jnp.int32, tail.shape, 0)
    top = jnp.where(row < shift, tail, rolled[:SUBLANES])
    return jnp.concatenate([top, rolled[SUBLANES:]], axis=0)


def _out_kernel(o_ref, za_ref, u_ref, gb_ref, gc_ref, zc_ref, ga_ref, gv_ref, x_ref, uh_ref, gch_ref,
                cw_ref, wpa_ref, wpc_ref, wo_ref, gain_ref, bias_ref, out_ref):
    ti = pl.program_id(1)
    o = jnp.concatenate([o_ref[0, p] for p in range(HEAD_PAIRS)], axis=1).astype(F32)
    za = za_ref[0].astype(F32)
    o_attn = o * (za * _sigmoid(za))

    cu = gc_ref[0].astype(F32) * u_ref[0].astype(F32)
    prev = gch_ref[0].astype(F32) * uh_ref[0].astype(F32)
    prev = jnp.where(ti > 0, prev, 0.0)
    cw = cw_ref[...]
    conv = cw[CONV_KERNEL - 1:CONV_KERNEL] * cu
    for tap in range(CONV_KERNEL - 1):
        conv = conv + cw[tap:tap + 1] * _shift_rows(cu, prev, CONV_KERNEL - 1 - tap)
    zc = zc_ref[0].astype(F32)
    o_conv = gb_ref[0].astype(F32) * conv * (zc * _sigmoid(zc))

    y_attn = jnp.dot(o_attn.astype(BF16), wpa_ref[...], preferred_element_type=F32)
    y_conv = jnp.dot(o_conv.astype(BF16), wpc_ref[...], preferred_element_type=F32)
    merged = _sigmoid(ga_ref[0].astype(F32)) * y_attn + _sigmoid(gv_ref[0].astype(F32)) * y_conv
    sub = jnp.dot(merged.astype(BF16), wo_ref[...], preferred_element_type=F32)

    h = DEEPNORM_ALPHA * x_ref[0] + sub
    mu = jnp.mean(h, axis=-1, keepdims=True)
    cen = h - mu
    var = jnp.mean(cen * cen, axis=-1, keepdims=True)
    out_ref[0] = cen * lax.rsqrt(var + LN_EPS) * gain_ref[...] + bias_ref[...]


def _out_stage(o4, za, u, gb, gc, zc, ga, gv, x, conv_w, wpa, wpc, wo, gain, bias):
    b, s, d = x.shape
    tm = ROW_TILE
    grid = (b, s // tm)
    halo_blocks = tm // SUBLANES
    half_spec = pl.BlockSpec((1, tm, SB_WIDTH), lambda bi, ti: (bi, ti, 0))
    full_spec = pl.BlockSpec((1, tm, D_MODEL), lambda bi, ti: (bi, ti, 0))
    halo_spec = pl.BlockSpec(
        (1, SUBLANES, CONV_WIDTH), lambda bi, ti: (bi, jnp.maximum(ti * halo_blocks - 1, 0), 0))

    def whole(a):
        return pl.BlockSpec(a.shape, lambda bi, ti: (0,) * a.ndim)

    return pl.pallas_call(
        _out_kernel,
        grid=grid,
        in_specs=[
            pl.BlockSpec((1, HEAD_PAIRS, tm, LANES), lambda bi, ti: (bi, 0, ti, 0)),
            half_spec, half_spec, half_spec, half_spec, half_spec,
            full_spec, full_spec, full_spec,
            halo_spec, halo_spec,
            whole(conv_w), whole(wpa), whole(wpc), whole(wo), whole(gain), whole(bias),
        ],
        out_specs=full_spec,
        out_shape=jax.ShapeDtypeStruct((b, s, d), x.dtype),
        compiler_params=pltpu.CompilerParams(
            dimension_semantics=("parallel", "parallel"), vmem_limit_bytes=VMEM_LIMIT_BYTES),
        name="out_stage",
    )(o4, za, u, gb, gc, zc, ga, gv, x, u, gc, conv_w, wpa, wpc, wo, gain, bias)


@jax.jit
def kernel(x, w_in, conv_w, w_proj_attn, w_proj_conv, w_out, ln_gain, ln_bias):
    h = x
    for layer in range(DEPTH):
        q4, k4, v4, za, u, gb, gc, zc, ga, gv = _in_proj(h, w_in[layer].astype(BF16))
        o4 = _attention(q4, k4, v4)
        h = _out_stage(
            o4, za, u, gb, gc, zc, ga, gv, h,
            conv_w[layer], w_proj_attn[layer].astype(BF16), w_proj_conv[layer].astype(BF16),
            w_out[layer].astype(BF16), ln_gain[layer][None, :], ln_bias[layer][None, :])
    return h
```

```python
import jax
import jax.numpy as jnp
from jax import lax
from jax.experimental import pallas as pl
from jax.experimental.pallas import tpu as pltpu

D_MODEL = 1024
SB_HEADS = 8
SB_HEAD_DIM = 64
SB_WIDTH = SB_HEADS * SB_HEAD_DIM
CONV_WIDTH = D_MODEL // 2
CONV_KERNEL = 3
DEPTH = 1
DEEPNORM_ALPHA = (2.0 * DEPTH) ** 0.25
LN_EPS = 1e-5

LANES = 128
SUBLANES = 8
HEAD_PAIRS = SB_WIDTH // LANES
BLOCK = 128
WINDOW_BLOCKS = 3
ROW_TILE = 512
VMEM_LIMIT_BYTES = 56 * 1024 * 1024
LOG_SURV_CUTOFF = -104.0

BF16 = jnp.bfloat16
F32 = jnp.float32


def _in_proj_kernel(x_ref, w_ref, q_ref, k_ref, v_ref, za_ref, u_ref, gb_ref, gc_ref, zc_ref,
                    ga_ref, gv_ref):
    xb = x_ref[0].astype(BF16)

    def seg(c0, width):
        return jnp.dot(xb, w_ref[:, c0:c0 + width], preferred_element_type=F32).astype(BF16)

    for idx, ref in enumerate((q_ref, k_ref, v_ref)):
        y = seg(idx * SB_WIDTH, SB_WIDTH)
        for p in range(HEAD_PAIRS):
            ref[0, p] = y[:, p * LANES:(p + 1) * LANES]
    c0 = 3 * SB_WIDTH
    for ref in (za_ref, u_ref, gb_ref, gc_ref, zc_ref):
        ref[0] = seg(c0, SB_WIDTH)
        c0 += SB_WIDTH
    for ref in (ga_ref, gv_ref):
        ref[0] = seg(c0, D_MODEL)
        c0 += D_MODEL


def _in_proj(x, w_in_bf16):
    b, s, d = x.shape
    tm = ROW_TILE
    grid = (b, s // tm)
    pair_shape = jax.ShapeDtypeStruct((b, HEAD_PAIRS, s, LANES), BF16)
    half_shape = jax.ShapeDtypeStruct((b, s, SB_WIDTH), BF16)
    full_shape = jax.ShapeDtypeStruct((b, s, D_MODEL), BF16)
    pair_spec = pl.BlockSpec((1, HEAD_PAIRS, tm, LANES), lambda bi, ti: (bi, 0, ti, 0))
    half_spec = pl.BlockSpec((1, tm, SB_WIDTH), lambda bi, ti: (bi, ti, 0))
    full_spec = pl.BlockSpec((1, tm, D_MODEL), lambda bi, ti: (bi, ti, 0))
    return pl.pallas_call(
        _in_proj_kernel,
        grid=grid,
        in_specs=[
            pl.BlockSpec((1, tm, d), lambda bi, ti: (bi, ti, 0)),
            pl.BlockSpec(w_in_bf16.shape, lambda bi, ti: (0, 0), pipeline_mode=pl.Buffered(1)),
        ],
        out_specs=[pair_spec] * 3 + [half_spec] * 5 + [full_spec] * 2,
        out_shape=[pair_shape] * 3 + [half_shape] * 5 + [full_shape] * 2,
        compiler_params=pltpu.CompilerParams(
            dimension_semantics=("parallel", "parallel"), vmem_limit_bytes=VMEM_LIMIT_BYTES),
        name="in_proj",
    )(x, w_in_bf16)


def _log_sigmoids(z):
    softplus = jnp.maximum(z, 0.0) + jnp.log(1.0 + jnp.exp(-jnp.abs(z)))
    return -softplus, z - softplus


def _hi_lo(x):
    hi = x.astype(BF16)
    lo = (x - hi.astype(F32)).astype(BF16)
    return jnp.concatenate([hi, lo], axis=1)


def _attn_kernel(q_ref, k_ref, v_ref, uu_ref, o_ref, qs_ref, acc_ref, ls_ref, flag_ref):
    s_len = q_ref.shape[2]
    nb = s_len // BLOCK
    lane = lax.broadcasted_iota(jnp.int32, (BLOCK, LANES), 1)
    first_head = lane < SB_HEAD_DIM
    row2 = lax.broadcasted_iota(jnp.int32, (2 * BLOCK, BLOCK), 0) & (BLOCK - 1)
    col2 = lax.broadcasted_iota(jnp.int32, (2 * BLOCK, BLOCK), 1)
    strict = col2 < row2

    def stacked_q(start):
        q2 = q_ref[0, 0, pl.ds(start, BLOCK), :] * jnp.asarray(SB_HEAD_DIM ** -0.5, BF16)
        zero = jnp.zeros_like(q2)
        return jnp.concatenate(
            [jnp.where(first_head, q2, zero), jnp.where(first_head, zero, q2)], axis=0)

    def store_out(start, acc):
        o_ref[0, 0, pl.ds(start, BLOCK), :] = jnp.where(
            first_head, acc[:BLOCK], acc[BLOCK:]).astype(o_ref.dtype)

    def window_block(i, carry):
        q_start = pl.multiple_of(i * BLOCK, BLOCK)
        w_start = pl.multiple_of((i - (WINDOW_BLOCKS - 1)) * BLOCK, BLOCK)
        kw = k_ref[0, 0, pl.ds(w_start, WINDOW_BLOCKS * BLOCK), :]
        vw = v_ref[0, 0, pl.ds(w_start, WINDOW_BLOCKS * BLOCK), :]
        z = lax.dot_general(stacked_q(q_start), kw, (((1,), (1,)), ((), ())),
                            preferred_element_type=F32)
        log_fail, log_hit = _log_sigmoids(z)
        lf = [log_fail[:, c * BLOCK:(c + 1) * BLOCK] for c in range(WINDOW_BLOCKS)]
        lf[-1] = jnp.where(strict, lf[-1], 0.0)
        newest_first = list(reversed(range(WINDOW_BLOCKS)))
        sums = jnp.dot(jnp.concatenate([_hi_lo(lf[c]) for c in newest_first], axis=0), uu_ref[...],
                       preferred_element_type=F32)
        log_surv = None
        weights = [None] * WINDOW_BLOCKS
        for n, c in enumerate(newest_first):
            block_sums = sums[n * 2 * BLOCK:(n + 1) * 2 * BLOCK]
            log_a = log_hit[:, c * BLOCK:(c + 1) * BLOCK] + block_sums[:, :BLOCK]
            if log_surv is not None:
                log_a = log_a + log_surv
            a = jnp.exp(log_a)
            if c == WINDOW_BLOCKS - 1:
                a = jnp.where(strict, a, 0.0)
            weights[c] = a.astype(BF16)
            total = block_sums[:, BLOCK:]
            log_surv = total if log_surv is None else log_surv + total
        store_out(q_start, jnp.dot(jnp.concatenate(weights, axis=1), vw, preferred_element_type=F32))
        flag_ref[i] = jnp.max(log_surv.reshape(2 * BLOCK // SUBLANES, SUBLANES, BLOCK), axis=0)
        return carry

    lax.fori_loop(WINDOW_BLOCKS - 1, nb, window_block, 0)

    def tile(j, masked):
        start = pl.multiple_of(j * BLOCK, BLOCK)
        k2 = k_ref[0, 0, pl.ds(start, BLOCK), :]
        v2 = v_ref[0, 0, pl.ds(start, BLOCK), :]
        z = lax.dot_general(qs_ref[...], k2, (((1,), (1,)), ((), ())),
                            preferred_element_type=F32)
        log_fail, log_hit = _log_sigmoids(z)
        if masked:
            log_fail = jnp.where(strict, log_fail, 0.0)
        sums = jnp.dot(_hi_lo(log_fail), uu_ref[...], preferred_element_type=F32)
        log_surv = ls_ref[...]
        a = jnp.exp(log_hit + sums[:, :BLOCK] + log_surv)
        if masked:
            a = jnp.where(strict, a, 0.0)
        acc_ref[...] += jnp.dot(a.astype(BF16), v2, preferred_element_type=F32)
        ls_ref[...] = log_surv + sums[:, BLOCK:]

    def general_block(i):
        start = pl.multiple_of(i * BLOCK, BLOCK)
        qs_ref[...] = stacked_q(start)
        acc_ref[...] = jnp.zeros_like(acc_ref)
        ls_ref[...] = jnp.zeros_like(ls_ref)
        tile(i, True)

        def more(c):
            j, worst = c
            return jnp.logical_and(j >= 0, worst > LOG_SURV_CUTOFF)

        def step(c):
            j, _ = c
            tile(j, False)
            return j - 1, jnp.max(ls_ref[...])

        lax.while_loop(more, step, (i - 1, jnp.max(ls_ref[...])))
        store_out(start, acc_ref[...])

    def head_block(i, carry):
        general_block(i)
        return carry

    lax.fori_loop(0, WINDOW_BLOCKS - 1, head_block, 0)

    @pl.when(jnp.max(flag_ref[WINDOW_BLOCKS - 1:]) > LOG_SURV_CUTOFF)
    def _():
        def maybe_redo(i, carry):
            @pl.when(jnp.max(flag_ref[i]) > LOG_SURV_CUTOFF)
            def _():
                general_block(i)
            return carry

        lax.fori_loop(WINDOW_BLOCKS - 1, nb, maybe_redo, 0)


def _cumsum_weights():
    key = jnp.arange(2 * BLOCK)[:, None] % BLOCK
    col = jnp.arange(2 * BLOCK)[None, :]
    return jnp.where((col >= BLOCK) | (key > col), 1.0, 0.0).astype(BF16)


def _attention(q4, k4, v4):
    b, pairs, s, _ = q4.shape
    assert s % BLOCK == 0 and s // BLOCK >= WINDOW_BLOCKS
    seq_spec = pl.BlockSpec((1, 1, s, LANES), lambda bi, pi: (bi, pi, 0, 0))
    uu = _cumsum_weights()
    return pl.pallas_call(
        _attn_kernel,
        grid=(b, pairs),
        in_specs=[seq_spec, seq_spec, seq_spec,
                  pl.BlockSpec(uu.shape, lambda bi, pi: (0, 0))],
        out_specs=seq_spec,
        out_shape=jax.ShapeDtypeStruct(q4.shape, BF16),
        scratch_shapes=[
            pltpu.VMEM((2 * BLOCK, LANES), BF16),
            pltpu.VMEM((2 * BLOCK, LANES), F32),
            pltpu.VMEM((2 * BLOCK, BLOCK), F32),
            pltpu.VMEM((s // BLOCK, SUBLANES, BLOCK), F32),
        ],
        compiler_params=pltpu.CompilerParams(
            dimension_semantics=("parallel", "parallel"), vmem_limit_bytes=VMEM_LIMIT_BYTES),
        name="stickbreak_attn",
    )(q4, k4, v4, uu)


def _sigmoid(x):
    return 1.0 / (1.0 + jnp.exp(-x))


def _shift_rows(cur, prev_tail, shift):
    rolled = pltpu.roll(cur, shift, 0)
    tail = pltpu.roll(prev_tail, shift, 0)
    row = lax.broadcasted_iota(jnp.int32, tail.shape, 0)
    top = jnp.where(row < shift, tail, rolled[:SUBLANES])
    return jnp.concatenate([top, rolled[SUBLANES:]], axis=0)


def _out_kernel(o_ref, za_ref, u_ref, gb_ref, gc_ref, zc_ref, ga_ref, gv_ref, x_ref, uh_ref, gch_ref,
                cw_ref, wpa_ref, wpc_ref, wo_ref, gain_ref, bias_ref, out_ref):
    ti = pl.program_id(1)
    o = jnp.concatenate([o_ref[0, p] for p in range(HEAD_PAIRS)], axis=1).astype(F32)
    za = za_ref[0].astype(F32)
    o_attn = o * (za * _sigmoid(za))

    cu = gc_ref[0].astype(F32) * u_ref[0].astype(F32)
    prev = gch_ref[0].astype(F32) * uh_ref[0].astype(F32)
    prev = jnp.where(ti > 0, prev, 0.0)
    cw = cw_ref[...]
    conv = cw[CONV_KERNEL - 1:CONV_KERNEL] * cu
    for tap in range(CONV_KERNEL - 1):
        conv = conv + cw[tap:tap + 1] * _shift_rows(cu, prev, CONV_KERNEL - 1 - tap)
    zc = zc_ref[0].astype(F32)
    o_conv = gb_ref[0].astype(F32) * conv * (zc * _sigmoid(zc))

    y_attn = jnp.dot(o_attn.astype(BF16), wpa_ref[...], preferred_element_type=F32)
    y_conv = jnp.dot(o_conv.astype(BF16), wpc_ref[...], preferred_element_type=F32)
    merged = _sigmoid(ga_ref[0].astype(F32)) * y_attn + _sigmoid(gv_ref[0].astype(F32)) * y_conv
    sub = jnp.dot(merged.astype(BF16), wo_ref[...], preferred_element_type=F32)

    h = DEEPNORM_ALPHA * x_ref[0] + sub
    mu = jnp.mean(h, axis=-1, keepdims=True)
    cen = h - mu
    var = jnp.mean(cen * cen, axis=-1, keepdims=True)
    out_ref[0] = cen * lax.rsqrt(var + LN_EPS) * gain_ref[...] + bias_ref[...]


def _out_stage(o4, za, u, gb, gc, zc, ga, gv, x, conv_w, wpa, wpc, wo, gain, bias):
    b, s, d = x.shape
    tm = ROW_TILE
    grid = (b, s // tm)
    halo_blocks = tm // SUBLANES
    half_spec = pl.BlockSpec((1, tm, SB_WIDTH), lambda bi, ti: (bi, ti, 0))
    full_spec = pl.BlockSpec((1, tm, D_MODEL), lambda bi, ti: (bi, ti, 0))
    halo_spec = pl.BlockSpec(
        (1, SUBLANES, CONV_WIDTH), lambda bi, ti: (bi, jnp.maximum(ti * halo_blocks - 1, 0), 0))

    def whole(a):
        return pl.BlockSpec(a.shape, lambda bi, ti: (0,) * a.ndim)

    return pl.pallas_call(
        _out_kernel,
        grid=grid,
        in_specs=[
            pl.BlockSpec((1, HEAD_PAIRS, tm, LANES), lambda bi, ti: (bi, 0, ti, 0)),
            half_spec, half_spec, half_spec, half_spec, half_spec,
            full_spec, full_spec, full_spec,
            halo_spec, halo_spec,
            whole(conv_w), whole(wpa), whole(wpc), whole(wo), whole(gain), whole(bias),
        ],
        out_specs=full_spec,
        out_shape=jax.ShapeDtypeStruct((b, s, d), x.dtype),
        compiler_params=pltpu.CompilerParams(
            dimension_semantics=("parallel", "parallel"), vmem_limit_bytes=VMEM_LIMIT_BYTES),
        name="out_stage",
    )(o4, za, u, gb, gc, zc, ga, gv, x, u, gc, conv_w, wpa, wpc, wo, gain, bias)


@jax.jit
def kernel(x, w_in, conv_w, w_proj_attn, w_proj_conv, w_out, ln_gain, ln_bias):
    h = x
    for layer in range(DEPTH):
        q4, k4, v4, za, u, gb, gc, zc, ga, gv = _in_proj(h, w_in[layer].astype(BF16))
        o4 = _attention(q4, k4, v4)
        h = _out_stage(
            o4, za, u, gb, gc, zc, ga, gv, h,
            conv_w[layer], w_proj_attn[layer].astype(BF16), w_proj_conv[layer].astype(BF16),
            w_out[layer].astype(BF16), ln_gain[layer][None, :], ln_bias[layer][None, :])
    return h
```

```python
import jax
import jax.numpy as jnp
from jax import lax
from jax.experimental import pallas as pl
from jax.experimental.pallas import tpu as pltpu

D_MODEL = 1024
SB_HEADS = 8
SB_HEAD_DIM = 64
SB_WIDTH = SB_HEADS * SB_HEAD_DIM
CONV_WIDTH = D_MODEL // 2
CONV_KERNEL = 3
DEPTH = 1
DEEPNORM_ALPHA = (2.0 * DEPTH) ** 0.25
LN_EPS = 1e-5

LANES = 128
SUBLANES = 8
HEAD_PAIRS = SB_WIDTH // LANES
BLOCK = 128
WINDOW_BLOCKS = 3
WINDOW_GROUP = 4
ROW_TILE = 512
VMEM_LIMIT_BYTES = 56 * 1024 * 1024
LOG_SURV_CUTOFF = -104.0

BF16 = jnp.bfloat16
F32 = jnp.float32


def _in_proj_kernel(x_ref, w_ref, q_ref, k_ref, v_ref, za_ref, u_ref, gb_ref, gc_ref, zc_ref,
                    ga_ref, gv_ref):
    xb = x_ref[0].astype(BF16)

    def seg(c0, width):
        return jnp.dot(xb, w_ref[:, c0:c0 + width], preferred_element_type=F32).astype(BF16)

    for idx, ref in enumerate((q_ref, k_ref, v_ref)):
        y = seg(idx * SB_WIDTH, SB_WIDTH)
        for p in range(HEAD_PAIRS):
            ref[0, p] = y[:, p * LANES:(p + 1) * LANES]
    c0 = 3 * SB_WIDTH
    for ref in (za_ref, u_ref, gb_ref, gc_ref, zc_ref):
        ref[0] = seg(c0, SB_WIDTH)
        c0 += SB_WIDTH
    for ref in (ga_ref, gv_ref):
        ref[0] = seg(c0, D_MODEL)
        c0 += D_MODEL


def _in_proj(x, w_in_bf16):
    b, s, d = x.shape
    tm = ROW_TILE
    grid = (b, s // tm)
    pair_shape = jax.ShapeDtypeStruct((b, HEAD_PAIRS, s, LANES), BF16)
    half_shape = jax.ShapeDtypeStruct((b, s, SB_WIDTH), BF16)
    full_shape = jax.ShapeDtypeStruct((b, s, D_MODEL), BF16)
    pair_spec = pl.BlockSpec((1, HEAD_PAIRS, tm, LANES), lambda bi, ti: (bi, 0, ti, 0))
    half_spec = pl.BlockSpec((1, tm, SB_WIDTH), lambda bi, ti: (bi, ti, 0))
    full_spec = pl.BlockSpec((1, tm, D_MODEL), lambda bi, ti: (bi, ti, 0))
    return pl.pallas_call(
        _in_proj_kernel,
        grid=grid,
        in_specs=[
            pl.BlockSpec((1, tm, d), lambda bi, ti: (bi, ti, 0)),
            pl.BlockSpec(w_in_bf16.shape, lambda bi, ti: (0, 0), pipeline_mode=pl.Buffered(1)),
        ],
        out_specs=[pair_spec] * 3 + [half_spec] * 5 + [full_spec] * 2,
        out_shape=[pair_shape] * 3 + [half_shape] * 5 + [full_shape] * 2,
        compiler_params=pltpu.CompilerParams(
            dimension_semantics=("parallel", "parallel"), vmem_limit_bytes=VMEM_LIMIT_BYTES),
        name="in_proj",
    )(x, w_in_bf16)


def _log_sigmoids(z):
    softplus = jnp.maximum(z, 0.0) + jnp.log(1.0 + jnp.exp(-jnp.abs(z)))
    return -softplus, z - softplus


def _hi_lo(x):
    hi = x.astype(BF16)
    lo = (x - hi.astype(F32)).astype(BF16)
    return jnp.concatenate([hi, lo], axis=1)


def _attn_kernel(q_ref, k_ref, v_ref, uu_ref, o_ref, qs_ref, acc_ref, ls_ref, flag_ref):
    s_len = q_ref.shape[2]
    nb = s_len // BLOCK
    lane = lax.broadcasted_iota(jnp.int32, (BLOCK, LANES), 1)
    first_head = lane < SB_HEAD_DIM
    row2 = lax.broadcasted_iota(jnp.int32, (2 * BLOCK, BLOCK), 0) & (BLOCK - 1)
    col2 = lax.broadcasted_iota(jnp.int32, (2 * BLOCK, BLOCK), 1)
    strict = col2 < row2

    def stacked_q(start):
        q2 = q_ref[0, 0, pl.ds(start, BLOCK), :] * jnp.asarray(SB_HEAD_DIM ** -0.5, BF16)
        zero = jnp.zeros_like(q2)
        return jnp.concatenate(
            [jnp.where(first_head, q2, zero), jnp.where(first_head, zero, q2)], axis=0)

    def store_out(start, acc):
        o_ref[0, 0, pl.ds(start, BLOCK), :] = jnp.where(
            first_head, acc[:BLOCK], acc[BLOCK:]).astype(o_ref.dtype)

    newest_first = list(reversed(range(WINDOW_BLOCKS)))

    def window_start(i):
        return pl.multiple_of((i - (WINDOW_BLOCKS - 1)) * BLOCK, BLOCK)

    def scores(i):
        kw = k_ref[0, 0, pl.ds(window_start(i), WINDOW_BLOCKS * BLOCK), :]
        return lax.dot_general(stacked_q(pl.multiple_of(i * BLOCK, BLOCK)), kw,
                               (((1,), (1,)), ((), ())), preferred_element_type=F32)

    def split_logs(z):
        log_fail, log_hit = _log_sigmoids(z)
        lf = [log_fail[:, c * BLOCK:(c + 1) * BLOCK] for c in range(WINDOW_BLOCKS)]
        lf[-1] = jnp.where(strict, lf[-1], 0.0)
        return jnp.concatenate([_hi_lo(lf[c]) for c in newest_first], axis=0), log_hit

    def later_sums(hl):
        return jnp.dot(hl, uu_ref[...], preferred_element_type=F32)

    def weights_of(sums, log_hit):
        log_surv = None
        weights = [None] * WINDOW_BLOCKS
        for n, c in enumerate(newest_first):
            block_sums = sums[n * 2 * BLOCK:(n + 1) * 2 * BLOCK]
            log_a = log_hit[:, c * BLOCK:(c + 1) * BLOCK] + block_sums[:, :BLOCK]
            if log_surv is not None:
                log_a = log_a + log_surv
            a = jnp.exp(log_a)
            if c == WINDOW_BLOCKS - 1:
                a = jnp.where(strict, a, 0.0)
            weights[c] = a.astype(BF16)
            total = block_sums[:, BLOCK:]
            log_surv = total if log_surv is None else log_surv + total
        return jnp.concatenate(weights, axis=1), log_surv

    def finish(i, weights, log_surv):
        vw = v_ref[0, 0, pl.ds(window_start(i), WINDOW_BLOCKS * BLOCK), :]
        store_out(pl.multiple_of(i * BLOCK, BLOCK),
                  jnp.dot(weights, vw, preferred_element_type=F32))
        flag_ref[i] = jnp.max(log_surv.reshape(2 * BLOCK // SUBLANES, SUBLANES, BLOCK), axis=0)

    def window_group(first, count):
        blocks = [first + n for n in range(count)]
        zs = [scores(i) for i in blocks]
        logs = [split_logs(z) for z in zs]
        sums = [later_sums(hl) for hl, _ in logs]
        probs = [weights_of(s, log_hit) for s, (_, log_hit) in zip(sums, logs)]
        for i, (w, log_surv) in zip(blocks, probs):
            finish(i, w, log_surv)

    def full_group(g, carry):
        window_group(WINDOW_BLOCKS - 1 + g * WINDOW_GROUP, WINDOW_GROUP)
        return carry

    n_window = nb - (WINDOW_BLOCKS - 1)
    lax.fori_loop(0, n_window // WINDOW_GROUP, full_group, 0)
    if n_window % WINDOW_GROUP:
        window_group(nb - n_window % WINDOW_GROUP, n_window % WINDOW_GROUP)

    def tile(j, masked):
        start = pl.multiple_of(j * BLOCK, BLOCK)
        k2 = k_ref[0, 0, pl.ds(start, BLOCK), :]
        v2 = v_ref[0, 0, pl.ds(start, BLOCK), :]
        z = lax.dot_general(qs_ref[...], k2, (((1,), (1,)), ((), ())),
                            preferred_element_type=F32)
        log_fail, log_hit = _log_sigmoids(z)
        if masked:
            log_fail = jnp.where(strict, log_fail, 0.0)
        sums = jnp.dot(_hi_lo(log_fail), uu_ref[...], preferred_element_type=F32)
        log_surv = ls_ref[...]
        a = jnp.exp(log_hit + sums[:, :BLOCK] + log_surv)
        if masked:
            a = jnp.where(strict, a, 0.0)
        acc_ref[...] += jnp.dot(a.astype(BF16), v2, preferred_element_type=F32)
        ls_ref[...] = log_surv + sums[:, BLOCK:]

    def general_block(i):
        start = pl.multiple_of(i * BLOCK, BLOCK)
        qs_ref[...] = stacked_q(start)
        acc_ref[...] = jnp.zeros_like(acc_ref)
        ls_ref[...] = jnp.zeros_like(ls_ref)
        tile(i, True)

        def more(c):
            j, worst = c
            return jnp.logical_and(j >= 0, worst > LOG_SURV_CUTOFF)

        def step(c):
            j, _ = c
            tile(j, False)
            return j - 1, jnp.max(ls_ref[...])

        lax.while_loop(more, step, (i - 1, jnp.max(ls_ref[...])))
        store_out(start, acc_ref[...])

    def head_block(i, carry):
        general_block(i)
        return carry

    lax.fori_loop(0, WINDOW_BLOCKS - 1, head_block, 0)

    @pl.when(jnp.max(flag_ref[WINDOW_BLOCKS - 1:]) > LOG_SURV_CUTOFF)
    def _():
        def maybe_redo(i, carry):
            @pl.when(jnp.max(flag_ref[i]) > LOG_SURV_CUTOFF)
            def _():
                general_block(i)
            return carry

        lax.fori_loop(WINDOW_BLOCKS - 1, nb, maybe_redo, 0)


def _cumsum_weights():
    key = jnp.arange(2 * BLOCK)[:, None] % BLOCK
    col = jnp.arange(2 * BLOCK)[None, :]
    return jnp.where((col >= BLOCK) | (key > col), 1.0, 0.0).astype(BF16)


def _attention(q4, k4, v4):
    b, pairs, s, _ = q4.shape
    assert s % BLOCK == 0 and s // BLOCK >= WINDOW_BLOCKS
    seq_spec = pl.BlockSpec((1, 1, s, LANES), lambda bi, pi: (bi, pi, 0, 0))
    uu = _cumsum_weights()
    return pl.pallas_call(
        _attn_kernel,
        grid=(b, pairs),
        in_specs=[seq_spec, seq_spec, seq_spec,
                  pl.BlockSpec(uu.shape, lambda bi, pi: (0, 0))],
        out_specs=seq_spec,
        out_shape=jax.ShapeDtypeStruct(q4.shape, BF16),
        scratch_shapes=[
            pltpu.VMEM((2 * BLOCK, LANES), BF16),
            pltpu.VMEM((2 * BLOCK, LANES), F32),
            pltpu.VMEM((2 * BLOCK, BLOCK), F32),
            pltpu.VMEM((s // BLOCK, SUBLANES, BLOCK), F32),
        ],
        compiler_params=pltpu.CompilerParams(
            dimension_semantics=("parallel", "parallel"), vmem_limit_bytes=VMEM_LIMIT_BYTES),
        name="stickbreak_attn",
    )(q4, k4, v4, uu)


def _sigmoid(x):
    return 1.0 / (1.0 + jnp.exp(-x))


def _shift_rows(cur, prev_tail, shift):
    rolled = pltpu.roll(cur, shift, 0)
    tail = pltpu.roll(prev_tail, shift, 0)
    row = lax.broadcasted_iota(jnp.int32, tail.shape, 0)
    top = jnp.where(row < shift, tail, rolled[:SUBLANES])
    return jnp.concatenate([top, rolled[SUBLANES:]], axis=0)


def _out_kernel(o_ref, za_ref, u_ref, gb_ref, gc_ref, zc_ref, ga_ref, gv_ref, x_ref, uh_ref, gch_ref,
                cw_ref, wpa_ref, wpc_ref, wo_ref, gain_ref, bias_ref, out_ref):
    ti = pl.program_id(1)
    o = jnp.concatenate([o_ref[0, p] for p in range(HEAD_PAIRS)], axis=1).astype(F32)
    za = za_ref[0].astype(F32)
    o_attn = o * (za * _sigmoid(za))

    cu = gc_ref[0].astype(F32) * u_ref[0].astype(F32)
    prev = gch_ref[0].astype(F32) * uh_ref[0].astype(F32)
    prev = jnp.where(ti > 0, prev, 0.0)
    cw = cw_ref[...]
    conv = cw[CONV_KERNEL - 1:CONV_KERNEL] * cu
    for tap in range(CONV_KERNEL - 1):
        conv = conv + cw[tap:tap + 1] * _shift_rows(cu, prev, CONV_KERNEL - 1 - tap)
    zc = zc_ref[0].astype(F32)
    o_conv = gb_ref[0].astype(F32) * conv * (zc * _sigmoid(zc))

    y_attn = jnp.dot(o_attn.astype(BF16), wpa_ref[...], preferred_element_type=F32)
    y_conv = jnp.dot(o_conv.astype(BF16), wpc_ref[...], preferred_element_type=F32)
    merged = _sigmoid(ga_ref[0].astype(F32)) * y_attn + _sigmoid(gv_ref[0].astype(F32)) * y_conv
    sub = jnp.dot(merged.astype(BF16), wo_ref[...], preferred_element_type=F32)

    h = DEEPNORM_ALPHA * x_ref[0] + sub
    mu = jnp.mean(h, axis=-1, keepdims=True)
    cen = h - mu
    var = jnp.mean(cen * cen, axis=-1, keepdims=True)
    out_ref[0] = cen * lax.rsqrt(var + LN_EPS) * gain_ref[...] + bias_ref[...]


def _out_stage(o4, za, u, gb, gc, zc, ga, gv, x, conv_w, wpa, wpc, wo, gain, bias):
    b, s, d = x.shape
    tm = ROW_TILE
    grid = (b, s // tm)
    halo_blocks = tm // SUBLANES
    half_spec = pl.BlockSpec((1, tm, SB_WIDTH), lambda bi, ti: (bi, ti, 0))
    full_spec = pl.BlockSpec((1, tm, D_MODEL), lambda bi, ti: (bi, ti, 0))
    halo_spec = pl.BlockSpec(
        (1, SUBLANES, CONV_WIDTH), lambda bi, ti: (bi, jnp.maximum(ti * halo_blocks - 1, 0), 0))

    def whole(a):
        return pl.BlockSpec(a.shape, lambda bi, ti: (0,) * a.ndim)

    return pl.pallas_call(
        _out_kernel,
        grid=grid,
        in_specs=[
            pl.BlockSpec((1, HEAD_PAIRS, tm, LANES), lambda bi, ti: (bi, 0, ti, 0)),
            half_spec, half_spec, half_spec, half_spec, half_spec,
            full_spec, full_spec, full_spec,
            halo_spec, halo_spec,
            whole(conv_w), whole(wpa), whole(wpc), whole(wo), whole(gain), whole(bias),
        ],
        out_specs=full_spec,
        out_shape=jax.ShapeDtypeStruct((b, s, d), x.dtype),
        compiler_params=pltpu.CompilerParams(
            dimension_semantics=("parallel", "parallel"), vmem_limit_bytes=VMEM_LIMIT_BYTES),
        name="out_stage",
    )(o4, za, u, gb, gc, zc, ga, gv, x, u, gc, conv_w, wpa, wpc, wo, gain, bias)


@jax.jit
def kernel(x, w_in, conv_w, w_proj_attn, w_proj_conv, w_out, ln_gain, ln_bias):
    h = x
    for layer in range(DEPTH):
        q4, k4, v4, za, u, gb, gc, zc, ga, gv = _in_proj(h, w_in[layer].astype(BF16))
        o4 = _attention(q4, k4, v4)
        h = _out_stage(
            o4, za, u, gb, gc, zc, ga, gv, h,
            conv_w[layer], w_proj_attn[layer].astype(BF16), w_proj_conv[layer].astype(BF16),
            w_out[layer].astype(BF16), ln_gain[layer][None, :], ln_bias[layer][None, :])
    return h
```

```python
import jax
import jax.numpy as jnp
from jax import lax
from jax.experimental import pallas as pl
from jax.experimental.pallas import tpu as pltpu

D_MODEL = 1024
SB_HEADS = 8
SB_HEAD_DIM = 64
SB_WIDTH = SB_HEADS * SB_HEAD_DIM
CONV_WIDTH = D_MODEL // 2
CONV_KERNEL = 3
DEPTH = 1
DEEPNORM_ALPHA = (2.0 * DEPTH) ** 0.25
LN_EPS = 1e-5

LANES = 128
SUBLANES = 8
HEAD_PAIRS = SB_WIDTH // LANES
BLOCK = 128
WINDOW_BLOCKS = 3
WINDOW_GROUP = 8
ROW_TILE = 512
VMEM_LIMIT_BYTES = 56 * 1024 * 1024
LOG2_E = 1.4426950408889634
LOG2_SURV_CUTOFF = -104.0 * LOG2_E
MASKED_SCORE = -1e4
LARGE = 1e30

BF16 = jnp.bfloat16
F32 = jnp.float32


def _in_proj_kernel(x_ref, w_ref, q_ref, k_ref, v_ref, za_ref, u_ref, gb_ref, gc_ref, zc_ref,
                    ga_ref, gv_ref):
    xb = x_ref[0].astype(BF16)

    def seg(c0, width, scale=None):
        y = jnp.dot(xb, w_ref[:, c0:c0 + width], preferred_element_type=F32)
        return (y if scale is None else y * scale).astype(BF16)

    q_scale = SB_HEAD_DIM ** -0.5 * LOG2_E
    for idx, ref in enumerate((q_ref, k_ref, v_ref)):
        y = seg(idx * SB_WIDTH, SB_WIDTH, q_scale if idx == 0 else None)
        for p in range(HEAD_PAIRS):
            ref[0, p] = y[:, p * LANES:(p + 1) * LANES]
    c0 = 3 * SB_WIDTH
    for ref in (za_ref, u_ref, gb_ref, gc_ref, zc_ref):
        ref[0] = seg(c0, SB_WIDTH)
        c0 += SB_WIDTH
    for ref in (ga_ref, gv_ref):
        ref[0] = seg(c0, D_MODEL)
        c0 += D_MODEL


def _in_proj(x, w_in_bf16):
    b, s, d = x.shape
    tm = ROW_TILE
    grid = (b, s // tm)
    pair_shape = jax.ShapeDtypeStruct((b, HEAD_PAIRS, s, LANES), BF16)
    half_shape = jax.ShapeDtypeStruct((b, s, SB_WIDTH), BF16)
    full_shape = jax.ShapeDtypeStruct((b, s, D_MODEL), BF16)
    pair_spec = pl.BlockSpec((1, HEAD_PAIRS, tm, LANES), lambda bi, ti: (bi, 0, ti, 0))
    half_spec = pl.BlockSpec((1, tm, SB_WIDTH), lambda bi, ti: (bi, ti, 0))
    full_spec = pl.BlockSpec((1, tm, D_MODEL), lambda bi, ti: (bi, ti, 0))
    return pl.pallas_call(
        _in_proj_kernel,
        grid=grid,
        in_specs=[
            pl.BlockSpec((1, tm, d), lambda bi, ti: (bi, ti, 0)),
            pl.BlockSpec(w_in_bf16.shape, lambda bi, ti: (0, 0), pipeline_mode=pl.Buffered(1)),
        ],
        out_specs=[pair_spec] * 3 + [half_spec] * 5 + [full_spec] * 2,
        out_shape=[pair_shape] * 3 + [half_shape] * 5 + [full_shape] * 2,
        compiler_params=pltpu.CompilerParams(
            dimension_semantics=("parallel", "parallel"), vmem_limit_bytes=VMEM_LIMIT_BYTES),
        name="in_proj",
    )(x, w_in_bf16)


def _softplus2(z):
    return jnp.maximum(z, 0.0) + jnp.log2(1.0 + jnp.exp2(-jnp.abs(z)))


def _attn_kernel(q_ref, k_ref, v_ref, uu_ref, o_ref, qs_ref, acc_ref, ls_ref, flag_ref):
    s_len = q_ref.shape[2]
    nb = s_len // BLOCK
    lane = lax.broadcasted_iota(jnp.int32, (BLOCK, LANES), 1)
    first_head = lane < SB_HEAD_DIM
    row2 = lax.broadcasted_iota(jnp.int32, (2 * BLOCK, BLOCK), 0) & (BLOCK - 1)
    col2 = lax.broadcasted_iota(jnp.int32, (2 * BLOCK, BLOCK), 1)
    strict = col2 < row2

    def offset(block):
        if isinstance(block, int):
            return block * BLOCK
        return pl.multiple_of(block * BLOCK, BLOCK)

    def stacked_q(i):
        q2 = q_ref[0, 0, pl.ds(offset(i), BLOCK), :]
        zero = jnp.zeros_like(q2)
        return jnp.concatenate(
            [jnp.where(first_head, q2, zero), jnp.where(first_head, zero, q2)], axis=0)

    def store_out(i, acc):
        o_ref[0, 0, pl.ds(offset(i), BLOCK), :] = jnp.where(
            first_head, acc[:BLOCK], acc[BLOCK:]).astype(o_ref.dtype)

    newest_first = list(reversed(range(WINDOW_BLOCKS)))

    def window_first(i):
        if isinstance(i, int):
            return max(i - (WINDOW_BLOCKS - 1), 0)
        return i - (WINDOW_BLOCKS - 1)

    def column_kinds(i):
        if not isinstance(i, int):
            return ['full'] * (WINDOW_BLOCKS - 1) + ['strict']
        first = window_first(i)
        return ['full' if first + c < i else 'strict' if first + c == i else 'none'
                for c in range(WINDOW_BLOCKS)]

    def scores(i):
        kw = k_ref[0, 0, pl.ds(offset(window_first(i)), WINDOW_BLOCKS * BLOCK), :]
        return lax.dot_general(stacked_q(i), kw, (((1,), (1,)), ((), ())),
                               preferred_element_type=F32)

    def split_logs(z, kinds):
        sp, log_hit = [], []
        for c, kind in enumerate(kinds):
            zc = z[:, c * BLOCK:(c + 1) * BLOCK]
            if kind == 'strict':
                zc = jnp.where(strict, zc, MASKED_SCORE)
            elif kind == 'none':
                zc = jnp.full_like(zc, MASKED_SCORE)
            softplus = _softplus2(zc)
            sp.append(softplus.astype(BF16))
            log_hit.append(zc - softplus)
        return jnp.concatenate([sp[c] for c in newest_first], axis=0), log_hit

    def later_sums(sp):
        return jnp.dot(sp, uu_ref[...], preferred_element_type=F32)

    def weights_of(sums, log_hit):
        log_surv = None
        weights = [None] * WINDOW_BLOCKS
        for n, c in enumerate(newest_first):
            block_sums = sums[n * 2 * BLOCK:(n + 1) * 2 * BLOCK]
            log_a = log_hit[c] + block_sums[:, :BLOCK]
            if log_surv is not None:
                log_a = log_a + log_surv
            weights[c] = jnp.exp2(log_a).astype(BF16)
            total = block_sums[:, BLOCK:]
            log_surv = total if log_surv is None else log_surv + total
        return jnp.concatenate(weights, axis=1), log_surv

    def finish(i, weights, log_surv):
        vw = v_ref[0, 0, pl.ds(offset(window_first(i)), WINDOW_BLOCKS * BLOCK), :]
        store_out(i, jnp.dot(weights, vw, preferred_element_type=F32))
        if isinstance(i, int) and i <= WINDOW_BLOCKS - 1:
            flag_ref[i] = jnp.full(flag_ref.shape[1:], -LARGE, F32)
        else:
            flag_ref[i] = jnp.max(
                log_surv.reshape(2 * BLOCK // SUBLANES, SUBLANES, BLOCK), axis=0)

    def window_group(blocks):
        zs = [scores(i) for i in blocks]
        logs = [split_logs(z, column_kinds(i)) for i, z in zip(blocks, zs)]
        sums = [later_sums(sp) for sp, _ in logs]
        probs = [weights_of(s, log_hit) for s, (_, log_hit) in zip(sums, logs)]
        for i, (w, log_surv) in zip(blocks, probs):
            finish(i, w, log_surv)

    def traced_group(g, carry):
        window_group([g * WINDOW_GROUP + n for n in range(WINDOW_GROUP)])
        return carry

    window_group(list(range(WINDOW_GROUP)))
    lax.fori_loop(1, nb // WINDOW_GROUP, traced_group, 0)

    def tile(j, masked):
        k2 = k_ref[0, 0, pl.ds(offset(j), BLOCK), :]
        v2 = v_ref[0, 0, pl.ds(offset(j), BLOCK), :]
        z = lax.dot_general(qs_ref[...], k2, (((1,), (1,)), ((), ())),
                            preferred_element_type=F32)
        if masked:
            z = jnp.where(strict, z, MASKED_SCORE)
        softplus = _softplus2(z)
        sums = jnp.dot(softplus.astype(BF16), uu_ref[...], preferred_element_type=F32)
        log_surv = ls_ref[...]
        a = jnp.exp2(z - softplus + sums[:, :BLOCK] + log_surv)
        acc_ref[...] += jnp.dot(a.astype(BF16), v2, preferred_element_type=F32)
        ls_ref[...] = log_surv + sums[:, BLOCK:]

    def general_block(i):
        qs_ref[...] = stacked_q(i)
        acc_ref[...] = jnp.zeros_like(acc_ref)
        ls_ref[...] = jnp.zeros_like(ls_ref)
        tile(i, True)

        def more(c):
            j, worst = c
            return jnp.logical_and(j >= 0, worst > LOG2_SURV_CUTOFF)

        def step(c):
            j, _ = c
            tile(j, False)
            return j - 1, jnp.max(ls_ref[...])

        lax.while_loop(more, step, (i - 1, jnp.max(ls_ref[...])))
        store_out(i, acc_ref[...])

    @pl.when(jnp.max(flag_ref[...]) > LOG2_SURV_CUTOFF)
    def _():
        def maybe_redo(i, carry):
            @pl.when(jnp.max(flag_ref[i]) > LOG2_SURV_CUTOFF)
            def _():
                general_block(i)
            return carry

        lax.fori_loop(0, nb, maybe_redo, 0)


def _cumsum_weights():
    key = jnp.arange(BLOCK)[:, None]
    col = jnp.arange(2 * BLOCK)[None, :]
    return jnp.where((col >= BLOCK) | (key > col), -1.0, 0.0).astype(BF16)


def _attention(q4, k4, v4):
    b, pairs, s, _ = q4.shape
    assert s % (BLOCK * WINDOW_GROUP) == 0 and WINDOW_GROUP >= WINDOW_BLOCKS
    seq_spec = pl.BlockSpec((1, 1, s, LANES), lambda bi, pi: (bi, pi, 0, 0))
    uu = _cumsum_weights()
    return pl.pallas_call(
        _attn_kernel,
        grid=(b, pairs),
        in_specs=[seq_spec, seq_spec, seq_spec,
                  pl.BlockSpec(uu.shape, lambda bi, pi: (0, 0))],
        out_specs=seq_spec,
        out_shape=jax.ShapeDtypeStruct(q4.shape, BF16),
        scratch_shapes=[
            pltpu.VMEM((2 * BLOCK, LANES), BF16),
            pltpu.VMEM((2 * BLOCK, LANES), F32),
            pltpu.VMEM((2 * BLOCK, BLOCK), F32),
            pltpu.VMEM((s // BLOCK, SUBLANES, BLOCK), F32),
        ],
        compiler_params=pltpu.CompilerParams(
            dimension_semantics=("parallel", "parallel"), vmem_limit_bytes=VMEM_LIMIT_BYTES),
        name="stickbreak_attn",
    )(q4, k4, v4, uu)


def _sigmoid(x):
    return 1.0 / (1.0 + jnp.exp(-x))


def _shift_rows(cur, prev_tail, shift):
    rolled = pltpu.roll(cur, shift, 0)
    tail = pltpu.roll(prev_tail, shift, 0)
    row = lax.broadcasted_iota(jnp.int32, tail.shape, 0)
    top = jnp.where(row < shift, tail, rolled[:SUBLANES])
    return jnp.concatenate([top, rolled[SUBLANES:]], axis=0)


def _out_kernel(o_ref, za_ref, u_ref, gb_ref, gc_ref, zc_ref, ga_ref, gv_ref, x_ref, uh_ref, gch_ref,
                cw_ref, wpa_ref, wpc_ref, wo_ref, gain_ref, bias_ref, out_ref):
    ti = pl.program_id(1)
    o = jnp.concatenate([o_ref[0, p] for p in range(HEAD_PAIRS)], axis=1).astype(F32)
    za = za_ref[0].astype(F32)
    o_attn = o * (za * _sigmoid(za))

    cu = gc_ref[0].astype(F32) * u_ref[0].astype(F32)
    prev = gch_ref[0].astype(F32) * uh_ref[0].astype(F32)
    prev = jnp.where(ti > 0, prev, 0.0)
    cw = cw_ref[...]
    conv = cw[CONV_KERNEL - 1:CONV_KERNEL] * cu
    for tap in range(CONV_KERNEL - 1):
        conv = conv + cw[tap:tap + 1] * _shift_rows(cu, prev, CONV_KERNEL - 1 - tap)
    zc = zc_ref[0].astype(F32)
    o_conv = gb_ref[0].astype(F32) * conv * (zc * _sigmoid(zc))

    y_attn = jnp.dot(o_attn.astype(BF16), wpa_ref[...], preferred_element_type=F32)
    y_conv = jnp.dot(o_conv.astype(BF16), wpc_ref[...], preferred_element_type=F32)
    merged = _sigmoid(ga_ref[0].astype(F32)) * y_attn + _sigmoid(gv_ref[0].astype(F32)) * y_conv
    sub = jnp.dot(merged.astype(BF16), wo_ref[...], preferred_element_type=F32)

    h = DEEPNORM_ALPHA * x_ref[0] + sub
    mu = jnp.mean(h, axis=-1, keepdims=True)
    cen = h - mu
    var = jnp.mean(cen * cen, axis=-1, keepdims=True)
    out_ref[0] = cen * lax.rsqrt(var + LN_EPS) * gain_ref[...] + bias_ref[...]


def _out_stage(o4, za, u, gb, gc, zc, ga, gv, x, conv_w, wpa, wpc, wo, gain, bias):
    b, s, d = x.shape
    tm = ROW_TILE
    grid = (b, s // tm)
    halo_blocks = tm // SUBLANES
    half_spec = pl.BlockSpec((1, tm, SB_WIDTH), lambda bi, ti: (bi, ti, 0))
    full_spec = pl.BlockSpec((1, tm, D_MODEL), lambda bi, ti: (bi, ti, 0))
    halo_spec = pl.BlockSpec(
        (1, SUBLANES, CONV_WIDTH), lambda bi, ti: (bi, jnp.maximum(ti * halo_blocks - 1, 0), 0))

    def whole(a):
        return pl.BlockSpec(a.shape, lambda bi, ti: (0,) * a.ndim)

    return pl.pallas_call(
        _out_kernel,
        grid=grid,
        in_specs=[
            pl.BlockSpec((1, HEAD_PAIRS, tm, LANES), lambda bi, ti: (bi, 0, ti, 0)),
            half_spec, half_spec, half_spec, half_spec, half_spec,
            full_spec, full_spec, full_spec,
            halo_spec, halo_spec,
            whole(conv_w), whole(wpa), whole(wpc), whole(wo), whole(gain), whole(bias),
        ],
        out_specs=full_spec,
        out_shape=jax.ShapeDtypeStruct((b, s, d), x.dtype),
        compiler_params=pltpu.CompilerParams(
            dimension_semantics=("parallel", "parallel"), vmem_limit_bytes=VMEM_LIMIT_BYTES),
        name="out_stage",
    )(o4, za, u, gb, gc, zc, ga, gv, x, u, gc, conv_w, wpa, wpc, wo, gain, bias)


@jax.jit
def kernel(x, w_in, conv_w, w_proj_attn, w_proj_conv, w_out, ln_gain, ln_bias):
    h = x
    for layer in range(DEPTH):
        q4, k4, v4, za, u, gb, gc, zc, ga, gv = _in_proj(h, w_in[layer].astype(BF16))
        o4 = _attention(q4, k4, v4)
        h = _out_stage(
            o4, za, u, gb, gc, zc, ga, gv, h,
            conv_w[layer], w_proj_attn[layer].astype(BF16), w_proj_conv[layer].astype(BF16),
            w_out[layer].astype(BF16), ln_gain[layer][None, :], ln_bias[layer][None, :])
    return h
```

```python
import jax
import jax.numpy as jnp
from jax import lax
from jax.experimental import pallas as pl
from jax.experimental.pallas import tpu as pltpu

D_MODEL = 1024
SB_HEADS = 8
SB_HEAD_DIM = 64
SB_WIDTH = SB_HEADS * SB_HEAD_DIM
CONV_WIDTH = D_MODEL // 2
CONV_KERNEL = 3
DEPTH = 1
DEEPNORM_ALPHA = (2.0 * DEPTH) ** 0.25
LN_EPS = 1e-5

LANES = 128
SUBLANES = 8
HEAD_PAIRS = SB_WIDTH // LANES
BLOCK = 128
WINDOW_BLOCKS = 3
WINDOW_GROUP = 8
ROW_TILE = 1024
OUT_SUBTILES = 4
VMEM_LIMIT_BYTES = 56 * 1024 * 1024
LOG2_E = 1.4426950408889634
LOG2_SURV_CUTOFF = -104.0 * LOG2_E
MASKED_SCORE = -1e4
LARGE = 1e30

BF16 = jnp.bfloat16
F32 = jnp.float32


def _sigmoid(x):
    return 1.0 / (1.0 + jnp.exp2(x * (-LOG2_E)))


def _silu(x):
    return x * _sigmoid(x)


def _in_proj_kernel(x_ref, w_ref, q_ref, k_ref, v_ref, sa_ref, cu_ref, gz_ref, sga_ref, sgv_ref):
    xb = x_ref[0].astype(BF16)

    def seg(index, width=SB_WIDTH):
        c0 = index * SB_WIDTH
        return jnp.dot(xb, w_ref[:, c0:c0 + width], preferred_element_type=F32)

    def store_pairs(ref, y):
        for p in range(HEAD_PAIRS):
            ref[0, p] = y[:, p * LANES:(p + 1) * LANES].astype(BF16)

    sga_ref[0] = _sigmoid(seg(8, D_MODEL)).astype(BF16)
    store_pairs(q_ref, seg(0) * (SB_HEAD_DIM ** -0.5 * LOG2_E))
    sgv_ref[0] = _sigmoid(seg(10, D_MODEL)).astype(BF16)
    store_pairs(k_ref, seg(1))
    sa_ref[0] = _silu(seg(3)).astype(BF16)
    store_pairs(v_ref, seg(2))
    gate_b = seg(5)
    gz_ref[0] = (gate_b * _silu(seg(7))).astype(BF16)
    u = seg(4)
    cu_ref[0] = (seg(6) * u).astype(BF16)


def _in_proj(x, w_in_bf16):
    b, s, d = x.shape
    tm = ROW_TILE
    grid = (b, s // tm)
    pair_shape = jax.ShapeDtypeStruct((b, HEAD_PAIRS, s, LANES), BF16)
    half_shape = jax.ShapeDtypeStruct((b, s, SB_WIDTH), BF16)
    full_shape = jax.ShapeDtypeStruct((b, s, D_MODEL), BF16)
    pair_spec = pl.BlockSpec((1, HEAD_PAIRS, tm, LANES), lambda bi, ti: (bi, 0, ti, 0))
    half_spec = pl.BlockSpec((1, tm, SB_WIDTH), lambda bi, ti: (bi, ti, 0))
    full_spec = pl.BlockSpec((1, tm, D_MODEL), lambda bi, ti: (bi, ti, 0))
    return pl.pallas_call(
        _in_proj_kernel,
        grid=grid,
        in_specs=[
            pl.BlockSpec((1, tm, d), lambda bi, ti: (bi, ti, 0)),
            pl.BlockSpec(w_in_bf16.shape, lambda bi, ti: (0, 0), pipeline_mode=pl.Buffered(1)),
        ],
        out_specs=[pair_spec] * 3 + [half_spec] * 3 + [full_spec] * 2,
        out_shape=[pair_shape] * 3 + [half_shape] * 3 + [full_shape] * 2,
        compiler_params=pltpu.CompilerParams(
            dimension_semantics=("parallel", "parallel"), vmem_limit_bytes=VMEM_LIMIT_BYTES),
        name="in_proj",
    )(x, w_in_bf16)


def _softplus2(z):
    return jnp.maximum(z, 0.0) + jnp.log2(1.0 + jnp.exp2(-jnp.abs(z)))


def _attn_kernel(q_ref, k_ref, v_ref, uu_ref, o_ref, qs_ref, acc_ref, ls_ref, flag_ref):
    s_len = q_ref.shape[2]
    nb = s_len // BLOCK
    lane = lax.broadcasted_iota(jnp.int32, (BLOCK, LANES), 1)
    first_head = lane < SB_HEAD_DIM
    row2 = lax.broadcasted_iota(jnp.int32, (2 * BLOCK, BLOCK), 0) & (BLOCK - 1)
    col2 = lax.broadcasted_iota(jnp.int32, (2 * BLOCK, BLOCK), 1)
    strict = col2 < row2

    def offset(block):
        if isinstance(block, int):
            return block * BLOCK
        return pl.multiple_of(block * BLOCK, BLOCK)

    def stacked_q(i):
        q2 = q_ref[0, 0, pl.ds(offset(i), BLOCK), :]
        zero = jnp.zeros_like(q2)
        return jnp.concatenate(
            [jnp.where(first_head, q2, zero), jnp.where(first_head, zero, q2)], axis=0)

    def store_out(i, acc):
        o_ref[0, 0, pl.ds(offset(i), BLOCK), :] = jnp.where(
            first_head, acc[:BLOCK], acc[BLOCK:]).astype(o_ref.dtype)

    newest_first = list(reversed(range(WINDOW_BLOCKS)))

    def window_first(i):
        if isinstance(i, int):
            return max(i - (WINDOW_BLOCKS - 1), 0)
        return i - (WINDOW_BLOCKS - 1)

    def column_kinds(i):
        if not isinstance(i, int):
            return ['full'] * (WINDOW_BLOCKS - 1) + ['strict']
        first = window_first(i)
        return ['full' if first + c < i else 'strict' if first + c == i else 'none'
                for c in range(WINDOW_BLOCKS)]

    def scores(i):
        kw = k_ref[0, 0, pl.ds(offset(window_first(i)), WINDOW_BLOCKS * BLOCK), :]
        return lax.dot_general(stacked_q(i), kw, (((1,), (1,)), ((), ())),
                               preferred_element_type=F32)

    def split_logs(z, kinds):
        sp, log_hit = [], []
        for c, kind in enumerate(kinds):
            zc = z[:, c * BLOCK:(c + 1) * BLOCK]
            if kind == 'strict':
                zc = jnp.where(strict, zc, MASKED_SCORE)
            elif kind == 'none':
                zc = jnp.full_like(zc, MASKED_SCORE)
            softplus = _softplus2(zc)
            sp.append(softplus.astype(BF16))
            log_hit.append(zc - softplus)
        return jnp.concatenate([sp[c] for c in newest_first], axis=0), log_hit

    def later_sums(sp):
        return jnp.dot(sp, uu_ref[...], preferred_element_type=F32)

    def weights_of(sums, log_hit):
        log_surv = None
        weights = [None] * WINDOW_BLOCKS
        for n, c in enumerate(newest_first):
            block_sums = sums[n * 2 * BLOCK:(n + 1) * 2 * BLOCK]
            log_a = log_hit[c] + block_sums[:, :BLOCK]
            if log_surv is not None:
                log_a = log_a + log_surv
            weights[c] = jnp.exp2(log_a).astype(BF16)
            total = block_sums[:, BLOCK:]
            log_surv = total if log_surv is None else log_surv + total
        return jnp.concatenate(weights, axis=1), log_surv

    def finish(i, weights, log_surv):
        vw = v_ref[0, 0, pl.ds(offset(window_first(i)), WINDOW_BLOCKS * BLOCK), :]
        store_out(i, jnp.dot(weights, vw, preferred_element_type=F32))
        if isinstance(i, int) and i <= WINDOW_BLOCKS - 1:
            flag_ref[i] = jnp.full(flag_ref.shape[1:], -LARGE, F32)
        else:
            flag_ref[i] = jnp.max(
                log_surv.reshape(2 * BLOCK // SUBLANES, SUBLANES, BLOCK), axis=0)

    def window_group(blocks):
        zs = [scores(i) for i in blocks]
        logs = [split_logs(z, column_kinds(i)) for i, z in zip(blocks, zs)]
        sums = [later_sums(sp) for sp, _ in logs]
        probs = [weights_of(s, log_hit) for s, (_, log_hit) in zip(sums, logs)]
        for i, (w, log_surv) in zip(blocks, probs):
            finish(i, w, log_surv)

    def traced_group(g, carry):
        window_group([g * WINDOW_GROUP + n for n in range(WINDOW_GROUP)])
        return carry

    window_group(list(range(WINDOW_GROUP)))
    lax.fori_loop(1, nb // WINDOW_GROUP, traced_group, 0)

    def tile(j, masked):
        k2 = k_ref[0, 0, pl.ds(offset(j), BLOCK), :]
        v2 = v_ref[0, 0, pl.ds(offset(j), BLOCK), :]
        z = lax.dot_general(qs_ref[...], k2, (((1,), (1,)), ((), ())),
                            preferred_element_type=F32)
        if masked:
            z = jnp.where(strict, z, MASKED_SCORE)
        softplus = _softplus2(z)
        sums = jnp.dot(softplus.astype(BF16), uu_ref[...], preferred_element_type=F32)
        log_surv = ls_ref[...]
        a = jnp.exp2(z - softplus + sums[:, :BLOCK] + log_surv)
        acc_ref[...] += jnp.dot(a.astype(BF16), v2, preferred_element_type=F32)
        ls_ref[...] = log_surv + sums[:, BLOCK:]

    def general_block(i):
        qs_ref[...] = stacked_q(i)
        acc_ref[...] = jnp.zeros_like(acc_ref)
        ls_ref[...] = jnp.zeros_like(ls_ref)
        tile(i, True)

        def more(c):
            j, worst = c
            return jnp.logical_and(j >= 0, worst > LOG2_SURV_CUTOFF)

        def step(c):
            j, _ = c
            tile(j, False)
            return j - 1, jnp.max(ls_ref[...])

        lax.while_loop(more, step, (i - 1, jnp.max(ls_ref[...])))
        store_out(i, acc_ref[...])

    @pl.when(jnp.max(flag_ref[...]) > LOG2_SURV_CUTOFF)
    def _():
        def maybe_redo(i, carry):
            @pl.when(jnp.max(flag_ref[i]) > LOG2_SURV_CUTOFF)
            def _():
                general_block(i)
            return carry

        lax.fori_loop(0, nb, maybe_redo, 0)


def _cumsum_weights():
    key = jnp.arange(BLOCK)[:, None]
    col = jnp.arange(2 * BLOCK)[None, :]
    return jnp.where((col >= BLOCK) | (key > col), -1.0, 0.0).astype(BF16)


def _attention(q4, k4, v4):
    b, pairs, s, _ = q4.shape
    assert s % (BLOCK * WINDOW_GROUP) == 0 and WINDOW_GROUP >= WINDOW_BLOCKS
    seq_spec = pl.BlockSpec((1, 1, s, LANES), lambda bi, pi: (bi, pi, 0, 0))
    uu = _cumsum_weights()
    return pl.pallas_call(
        _attn_kernel,
        grid=(b, pairs),
        in_specs=[seq_spec, seq_spec, seq_spec,
                  pl.BlockSpec(uu.shape, lambda bi, pi: (0, 0))],
        out_specs=seq_spec,
        out_shape=jax.ShapeDtypeStruct(q4.shape, BF16),
        scratch_shapes=[
            pltpu.VMEM((2 * BLOCK, LANES), BF16),
            pltpu.VMEM((2 * BLOCK, LANES), F32),
            pltpu.VMEM((2 * BLOCK, BLOCK), F32),
            pltpu.VMEM((s // BLOCK, SUBLANES, BLOCK), F32),
        ],
        compiler_params=pltpu.CompilerParams(
            dimension_semantics=("parallel", "parallel"), vmem_limit_bytes=VMEM_LIMIT_BYTES),
        name="stickbreak_attn",
    )(q4, k4, v4, uu)


def _shift_rows(cur, prev_tail, shift):
    rolled = pltpu.roll(cur, shift, 0)
    tail = pltpu.roll(prev_tail, shift, 0)
    row = lax.broadcasted_iota(jnp.int32, tail.shape, 0)
    top = jnp.where(row < shift, tail, rolled[:SUBLANES])
    return jnp.concatenate([top, rolled[SUBLANES:]], axis=0)


def _out_kernel(o_ref, sa_ref, cu_ref, gz_ref, sga_ref, sgv_ref, x_ref, cuh_ref,
                cw_ref, wpa_ref, wpc_ref, wo_ref, gain_ref, bias_ref, out_ref):
    ti = pl.program_id(1)
    rows = out_ref.shape[1] // OUT_SUBTILES
    cw = cw_ref[...]

    def branch_inputs(n):
        lo = n * rows
        o = jnp.concatenate(
            [o_ref[0, p, lo:lo + rows, :] for p in range(HEAD_PAIRS)], axis=1).astype(F32)
        o_attn = o * sa_ref[0, lo:lo + rows, :].astype(F32)
        cu = cu_ref[0, lo:lo + rows, :].astype(F32)
        if n == 0:
            prev = jnp.where(ti > 0, cuh_ref[0].astype(F32), 0.0)
        else:
            prev = cu_ref[0, lo - SUBLANES:lo, :].astype(F32)
        conv = cw[CONV_KERNEL - 1:CONV_KERNEL] * cu
        for tap in range(CONV_KERNEL - 1):
            conv = conv + cw[tap:tap + 1] * _shift_rows(cu, prev, CONV_KERNEL - 1 - tap)
        o_conv = gz_ref[0, lo:lo + rows, :].astype(F32) * conv
        return o_attn.astype(BF16), o_conv.astype(BF16)

    def merge(n, y_attn, y_conv):
        lo = n * rows
        merged = (sga_ref[0, lo:lo + rows, :].astype(F32) * y_attn
                  + sgv_ref[0, lo:lo + rows, :].astype(F32) * y_conv)
        return merged.astype(BF16)

    def finish(n, sub):
        lo = n * rows
        h = DEEPNORM_ALPHA * x_ref[0, lo:lo + rows, :] + sub
        mu = jnp.mean(h, axis=-1, keepdims=True)
        cen = h - mu
        var = jnp.mean(cen * cen, axis=-1, keepdims=True)
        out_ref[0, lo:lo + rows, :] = cen * lax.rsqrt(var + LN_EPS) * gain_ref[...] + bias_ref[...]

    subtiles = range(OUT_SUBTILES)
    branches = [branch_inputs(n) for n in subtiles]
    projected = [(jnp.dot(a, wpa_ref[...], preferred_element_type=F32),
                  jnp.dot(c, wpc_ref[...], preferred_element_type=F32)) for a, c in branches]
    merged = [merge(n, ya, yc) for n, (ya, yc) in zip(subtiles, projected)]
    subs = [jnp.dot(m, wo_ref[...], preferred_element_type=F32) for m in merged]
    for n, sub in zip(subtiles, subs):
        finish(n, sub)


def _out_stage(o4, sa, cu, gz, sga, sgv, x, conv_w, wpa, wpc, wo, gain, bias):
    b, s, d = x.shape
    tm = ROW_TILE
    grid = (b, s // tm)
    halo_blocks = tm // SUBLANES
    half_spec = pl.BlockSpec((1, tm, SB_WIDTH), lambda bi, ti: (bi, ti, 0))
    full_spec = pl.BlockSpec((1, tm, D_MODEL), lambda bi, ti: (bi, ti, 0))
    halo_spec = pl.BlockSpec(
        (1, SUBLANES, CONV_WIDTH), lambda bi, ti: (bi, jnp.maximum(ti * halo_blocks - 1, 0), 0))

    def whole(a):
        return pl.BlockSpec(a.shape, lambda bi, ti: (0,) * a.ndim)

    return pl.pallas_call(
        _out_kernel,
        grid=grid,
        in_specs=[
            pl.BlockSpec((1, HEAD_PAIRS, tm, LANES), lambda bi, ti: (bi, 0, ti, 0)),
            half_spec, half_spec, half_spec,
            full_spec, full_spec, full_spec,
            halo_spec,
            whole(conv_w), whole(wpa), whole(wpc), whole(wo), whole(gain), whole(bias),
        ],
        out_specs=full_spec,
        out_shape=jax.ShapeDtypeStruct((b, s, d), x.dtype),
        compiler_params=pltpu.CompilerParams(
            dimension_semantics=("parallel", "parallel"), vmem_limit_bytes=VMEM_LIMIT_BYTES),
        name="out_stage",
    )(o4, sa, cu, gz, sga, sgv, x, cu, conv_w, wpa, wpc, wo, gain, bias)


@jax.jit
def kernel(x, w_in, conv_w, w_proj_attn, w_proj_conv, w_out, ln_gain, ln_bias):
    h = x
    for layer in range(DEPTH):
        q4, k4, v4, sa, cu, gz, sga, sgv = _in_proj(h, w_in[layer].astype(BF16))
        o4 = _attention(q4, k4, v4)
        h = _out_stage(
            o4, sa, cu, gz, sga, sgv, h,
            conv_w[layer], w_proj_attn[layer].astype(BF16), w_proj_conv[layer].astype(BF16),
            w_out[layer].astype(BF16), ln_gain[layer][None, :], ln_bias[layer][None, :])
    return h
```

```python
import jax
import jax.numpy as jnp
from jax import lax
from jax.experimental import pallas as pl
from jax.experimental.pallas import tpu as pltpu

D_MODEL = 1024
SB_HEADS = 8
SB_HEAD_DIM = 64
SB_WIDTH = SB_HEADS * SB_HEAD_DIM
CONV_WIDTH = D_MODEL // 2
CONV_KERNEL = 3
DEPTH = 1
DEEPNORM_ALPHA = (2.0 * DEPTH) ** 0.25
LN_EPS = 1e-5

LANES = 128
SUBLANES = 8
HEAD_PAIRS = SB_WIDTH // LANES
BLOCK = 128
WINDOW_BLOCKS = 3
WINDOW_GROUP = 8
IN_ROW_TILE = 512
OUT_ROW_TILE = 1024
OUT_SUBTILES = 4
VMEM_LIMIT_BYTES = 56 * 1024 * 1024
LOG2_E = 1.4426950408889634
LOG2_SURV_CUTOFF = -104.0 * LOG2_E
MASKED_SCORE = -1e4
LARGE = 1e30

BF16 = jnp.bfloat16
F32 = jnp.float32


def _sigmoid(x):
    return 1.0 / (1.0 + jnp.exp2(x * (-LOG2_E)))


def _silu(x):
    return x * _sigmoid(x)


def _in_proj_kernel(x_ref, w_ref, q_ref, k_ref, v_ref, sa_ref, cu_ref, gz_ref, sga_ref, sgv_ref):
    xb = x_ref[0].astype(BF16)

    def seg(index, width=SB_WIDTH):
        c0 = index * SB_WIDTH
        return jnp.dot(xb, w_ref[:, c0:c0 + width], preferred_element_type=F32)

    def store_pairs(ref, y):
        for p in range(HEAD_PAIRS):
            ref[0, p] = y[:, p * LANES:(p + 1) * LANES].astype(BF16)

    sga_ref[0] = _sigmoid(seg(8, D_MODEL)).astype(BF16)
    store_pairs(q_ref, seg(0) * (SB_HEAD_DIM ** -0.5 * LOG2_E))
    sgv_ref[0] = _sigmoid(seg(10, D_MODEL)).astype(BF16)
    store_pairs(k_ref, seg(1))
    sa_ref[0] = _silu(seg(3)).astype(BF16)
    store_pairs(v_ref, seg(2))
    gate_b = seg(5)
    gz_ref[0] = (gate_b * _silu(seg(7))).astype(BF16)
    u = seg(4)
    cu_ref[0] = (seg(6) * u).astype(BF16)


def _in_proj(x, w_in_bf16):
    b, s, d = x.shape
    tm = IN_ROW_TILE
    grid = (b, s // tm)
    pair_shape = jax.ShapeDtypeStruct((b, HEAD_PAIRS, s, LANES), BF16)
    half_shape = jax.ShapeDtypeStruct((b, s, SB_WIDTH), BF16)
    full_shape = jax.ShapeDtypeStruct((b, s, D_MODEL), BF16)
    pair_spec = pl.BlockSpec((1, HEAD_PAIRS, tm, LANES), lambda bi, ti: (bi, 0, ti, 0))
    half_spec = pl.BlockSpec((1, tm, SB_WIDTH), lambda bi, ti: (bi, ti, 0))
    full_spec = pl.BlockSpec((1, tm, D_MODEL), lambda bi, ti: (bi, ti, 0))
    return pl.pallas_call(
        _in_proj_kernel,
        grid=grid,
        in_specs=[
            pl.BlockSpec((1, tm, d), lambda bi, ti: (bi, ti, 0)),
            pl.BlockSpec(w_in_bf16.shape, lambda bi, ti: (0, 0), pipeline_mode=pl.Buffered(1)),
        ],
        out_specs=[pair_spec] * 3 + [half_spec] * 3 + [full_spec] * 2,
        out_shape=[pair_shape] * 3 + [half_shape] * 3 + [full_shape] * 2,
        compiler_params=pltpu.CompilerParams(
            dimension_semantics=("parallel", "parallel"), vmem_limit_bytes=VMEM_LIMIT_BYTES),
        name="in_proj",
    )(x, w_in_bf16)


def _softplus2(z):
    return jnp.maximum(z, 0.0) + jnp.log2(1.0 + jnp.exp2(-jnp.abs(z)))


def _attn_kernel(q_ref, k_ref, v_ref, uu_ref, o_ref, qs_ref, acc_ref, ls_ref, flag_ref):
    s_len = q_ref.shape[2]
    nb = s_len // BLOCK
    lane = lax.broadcasted_iota(jnp.int32, (BLOCK, LANES), 1)
    first_head = lane < SB_HEAD_DIM
    row2 = lax.broadcasted_iota(jnp.int32, (2 * BLOCK, BLOCK), 0) & (BLOCK - 1)
    col2 = lax.broadcasted_iota(jnp.int32, (2 * BLOCK, BLOCK), 1)
    strict = col2 < row2

    def offset(block):
        if isinstance(block, int):
            return block * BLOCK
        return pl.multiple_of(block * BLOCK, BLOCK)

    def stacked_q(i):
        q2 = q_ref[0, 0, pl.ds(offset(i), BLOCK), :]
        zero = jnp.zeros_like(q2)
        return jnp.concatenate(
            [jnp.where(first_head, q2, zero), jnp.where(first_head, zero, q2)], axis=0)

    def store_out(i, acc):
        o_ref[0, 0, pl.ds(offset(i), BLOCK), :] = jnp.where(
            first_head, acc[:BLOCK], acc[BLOCK:]).astype(o_ref.dtype)

    newest_first = list(reversed(range(WINDOW_BLOCKS)))

    def window_first(i):
        if isinstance(i, int):
            return max(i - (WINDOW_BLOCKS - 1), 0)
        return i - (WINDOW_BLOCKS - 1)

    def column_kinds(i):
        if not isinstance(i, int):
            return ['full'] * (WINDOW_BLOCKS - 1) + ['strict']
        first = window_first(i)
        return ['full' if first + c < i else 'strict' if first + c == i else 'none'
                for c in range(WINDOW_BLOCKS)]

    def scores(i):
        kw = k_ref[0, 0, pl.ds(offset(window_first(i)), WINDOW_BLOCKS * BLOCK), :]
        return lax.dot_general(stacked_q(i), kw, (((1,), (1,)), ((), ())),
                               preferred_element_type=F32)

    def split_logs(z, kinds):
        sp, log_hit = [], []
        for c, kind in enumerate(kinds):
            zc = z[:, c * BLOCK:(c + 1) * BLOCK]
            if kind == 'strict':
                zc = jnp.where(strict, zc, MASKED_SCORE)
            elif kind == 'none':
                zc = jnp.full_like(zc, MASKED_SCORE)
            softplus = _softplus2(zc)
            sp.append(softplus.astype(BF16))
            log_hit.append(zc - softplus)
        return jnp.concatenate([sp[c] for c in newest_first], axis=0), log_hit

    def later_sums(sp):
        return jnp.dot(sp, uu_ref[...], preferred_element_type=F32)

    def weights_of(sums, log_hit):
        log_surv = None
        weights = [None] * WINDOW_BLOCKS
        for n, c in enumerate(newest_first):
            block_sums = sums[n * 2 * BLOCK:(n + 1) * 2 * BLOCK]
            log_a = log_hit[c] + block_sums[:, :BLOCK]
            if log_surv is not None:
                log_a = log_a + log_surv
            weights[c] = jnp.exp2(log_a).astype(BF16)
            total = block_sums[:, BLOCK:]
            log_surv = total if log_surv is None else log_surv + total
        return jnp.concatenate(weights, axis=1), log_surv

    def finish(i, weights, log_surv):
        vw = v_ref[0, 0, pl.ds(offset(window_first(i)), WINDOW_BLOCKS * BLOCK), :]
        store_out(i, jnp.dot(weights, vw, preferred_element_type=F32))
        if isinstance(i, int) and i <= WINDOW_BLOCKS - 1:
            flag_ref[i] = jnp.full(flag_ref.shape[1:], -LARGE, F32)
        else:
            flag_ref[i] = jnp.max(
                log_surv.reshape(2 * BLOCK // SUBLANES, SUBLANES, BLOCK), axis=0)

    def group_blocks(g):
        return [g * WINDOW_GROUP + n for n in range(WINDOW_GROUP)]

    def window_group(blocks):
        zs = [scores(i) for i in blocks]
        logs = [split_logs(z, column_kinds(i)) for i, z in zip(blocks, zs)]
        sums = [later_sums(sp) for sp, _ in logs]
        probs = [weights_of(s, log_hit) for s, (_, log_hit) in zip(sums, logs)]
        for i, (w, log_surv) in zip(blocks, probs):
            finish(i, w, log_surv)

    def traced_group(g, carry):
        window_group(group_blocks(g))
        return carry

    window_group(group_blocks(0))
    lax.fori_loop(1, nb // WINDOW_GROUP, traced_group, 0)

    def tile(j, masked):
        k2 = k_ref[0, 0, pl.ds(offset(j), BLOCK), :]
        v2 = v_ref[0, 0, pl.ds(offset(j), BLOCK), :]
        z = lax.dot_general(qs_ref[...], k2, (((1,), (1,)), ((), ())),
                            preferred_element_type=F32)
        if masked:
            z = jnp.where(strict, z, MASKED_SCORE)
        softplus = _softplus2(z)
        sums = jnp.dot(softplus.astype(BF16), uu_ref[...], preferred_element_type=F32)
        log_surv = ls_ref[...]
        a = jnp.exp2(z - softplus + sums[:, :BLOCK] + log_surv)
        acc_ref[...] += jnp.dot(a.astype(BF16), v2, preferred_element_type=F32)
        ls_ref[...] = log_surv + sums[:, BLOCK:]

    def general_block(i):
        qs_ref[...] = stacked_q(i)
        acc_ref[...] = jnp.zeros_like(acc_ref)
        ls_ref[...] = jnp.zeros_like(ls_ref)
        tile(i, True)

        def more(c):
            j, worst = c
            return jnp.logical_and(j >= 0, worst > LOG2_SURV_CUTOFF)

        def step(c):
            j, _ = c
            tile(j, False)
            return j - 1, jnp.max(ls_ref[...])

        lax.while_loop(more, step, (i - 1, jnp.max(ls_ref[...])))
        store_out(i, acc_ref[...])

    @pl.when(jnp.max(flag_ref[...]) > LOG2_SURV_CUTOFF)
    def _():
        def maybe_redo(i, carry):
            @pl.when(jnp.max(flag_ref[i]) > LOG2_SURV_CUTOFF)
            def _():
                general_block(i)
            return carry

        lax.fori_loop(0, nb, maybe_redo, 0)


def _cumsum_weights():
    key = jnp.arange(BLOCK)[:, None]
    col = jnp.arange(2 * BLOCK)[None, :]
    return jnp.where((col >= BLOCK) | (key > col), -1.0, 0.0).astype(BF16)


def _attention(q4, k4, v4):
    b, pairs, s, _ = q4.shape
    assert s % (BLOCK * WINDOW_GROUP) == 0 and WINDOW_GROUP >= WINDOW_BLOCKS
    seq_spec =pl.BlockSpec((1, 1, s, LANES), lambda bi, pi: (bi, pi, 0, 0))
    uu = _cumsum_weights()
    return pl.pallas_call(
        _attn_kernel,
        grid=(b, pairs),
        in_specs=[seq_spec, seq_spec, seq_spec,
                  pl.BlockSpec(uu.shape, lambda bi, pi: (0, 0))],
        out_specs=seq_spec,
        out_shape=jax.ShapeDtypeStruct(q4.shape, BF16),
        scratch_shapes=[
            pltpu.VMEM((2 * BLOCK, LANES), BF16),
            pltpu.VMEM((2 * BLOCK, LANES), F32),
            pltpu.VMEM((2 * BLOCK, BLOCK), F32),
            pltpu.VMEM((s // BLOCK, SUBLANES, BLOCK), F32),
        ],
        compiler_params=pltpu.CompilerParams(
            dimension_semantics=("parallel", "parallel"), vmem_limit_bytes=VMEM_LIMIT_BYTES),
        name="stickbreak_attn",
    )(q4, k4, v4, uu)


def _shift_rows(cur, prev_tail, shift):
    rolled = pltpu.roll(cur, shift, 0)
    tail = pltpu.roll(prev_tail, shift, 0)
    row = lax.broadcasted_iota(jnp.int32, tail.shape, 0)
    top = jnp.where(row < shift, tail, rolled[:SUBLANES])
    return jnp.concatenate([top, rolled[SUBLANES:]], axis=0)


def _out_kernel(o_ref, sa_ref, cu_ref, gz_ref, sga_ref, sgv_ref, x_ref, cuh_ref,
                cw_ref, wpa_ref, wpc_ref, wo_ref, gain_ref, bias_ref, out_ref):
    ti = pl.program_id(1)
    rows = out_ref.shape[1] // OUT_SUBTILES
    cw = cw_ref[...]

    def branch_inputs(n):
        lo = n * rows
        o = jnp.concatenate(
            [o_ref[0, p, lo:lo + rows, :] for p in range(HEAD_PAIRS)], axis=1).astype(F32)
        o_attn = o * sa_ref[0, lo:lo + rows, :].astype(F32)
        cu = cu_ref[0, lo:lo + rows, :].astype(F32)
        if n == 0:
            prev = jnp.where(ti > 0, cuh_ref[0].astype(F32), 0.0)
        else:
            prev = cu_ref[0, lo - SUBLANES:lo, :].astype(F32)
        conv = cw[CONV_KERNEL - 1:CONV_KERNEL] * cu
        for tap in range(CONV_KERNEL - 1):
            conv = conv + cw[tap:tap + 1] * _shift_rows(cu, prev, CONV_KERNEL - 1 - tap)
        o_conv = gz_ref[0, lo:lo + rows, :].astype(F32) * conv
        return o_attn.astype(BF16), o_conv.astype(BF16)

    def merge(n, y_attn, y_conv):
        lo = n * rows
        merged = (sga_ref[0, lo:lo + rows, :].astype(F32) * y_attn
                  + sgv_ref[0, lo:lo + rows, :].astype(F32) * y_conv)
        return merged.astype(BF16)

    def finish(n, sub):
        lo = n * rows
        h = DEEPNORM_ALPHA * x_ref[0, lo:lo + rows, :] + sub
        mu = jnp.mean(h, axis=-1, keepdims=True)
        cen = h - mu
        var = jnp.mean(cen * cen, axis=-1, keepdims=True)
        out_ref[0, lo:lo + rows, :] = cen * lax.rsqrt(var + LN_EPS) * gain_ref[...] + bias_ref[...]

    subtiles = range(OUT_SUBTILES)
    branches = [branch_inputs(n) for n in subtiles]
    projected = [(jnp.dot(a, wpa_ref[...], preferred_element_type=F32),
                  jnp.dot(c, wpc_ref[...], preferred_element_type=F32)) for a, c in branches]
    merged = [merge(n, ya, yc) for n, (ya, yc) in zip(subtiles, projected)]
    subs = [jnp.dot(m, wo_ref[...], preferred_element_type=F32) for m in merged]
    for n, sub in zip(subtiles, subs):
        finish(n, sub)


def _out_stage(o4, sa, cu, gz, sga, sgv, x, conv_w, wpa, wpc, wo, gain, bias):
    b, s, d = x.shape
    tm = OUT_ROW_TILE
    grid = (b, s // tm)
    halo_blocks = tm // SUBLANES
    half_spec = pl.BlockSpec((1, tm, SB_WIDTH), lambda bi, ti: (bi, ti, 0))
    full_spec = pl.BlockSpec((1, tm, D_MODEL), lambda bi, ti: (bi, ti, 0))
    halo_spec = pl.BlockSpec(
        (1, SUBLANES, CONV_WIDTH), lambda bi, ti: (bi, jnp.maximum(ti * halo_blocks - 1, 0), 0))

    def whole(a):
        return pl.BlockSpec(a.shape, lambda bi, ti: (0,) * a.ndim)

    return pl.pallas_call(
        _out_kernel,
        grid=grid,
        in_specs=[
            pl.BlockSpec((1, HEAD_PAIRS, tm, LANES), lambda bi, ti: (bi, 0, ti, 0)),
            half_spec, half_spec, half_spec,
            full_spec, full_spec, full_spec,
            halo_spec,
            whole(conv_w), whole(wpa), whole(wpc), whole(wo), whole(gain), whole(bias),
        ],
        out_specs=full_spec,
        out_shape=jax.ShapeDtypeStruct((b, s, d), x.dtype),
        compiler_params=pltpu.CompilerParams(
            dimension_semantics=("parallel", "parallel"), vmem_limit_bytes=VMEM_LIMIT_BYTES),
        name="out_stage",
    )(o4, sa, cu, gz, sga, sgv, x, cu, conv_w, wpa, wpc, wo, gain, bias)


@jax.jit
def kernel(x, w_in, conv_w, w_proj_attn, w_proj_conv, w_out, ln_gain, ln_bias):
    h = x
    for layer in range(DEPTH):
        q4, k4, v4, sa, cu, gz, sga, sgv = _in_proj(h, w_in[layer].astype(BF16))
        o4 = _attention(q4, k4, v4)
        h = _out_stage(
            o4, sa, cu, gz, sga, sgv, h,
            conv_w[layer], w_proj_attn[layer].astype(BF16), w_proj_conv[layer].astype(BF16),
            w_out[layer].astype(BF16), ln_gain[layer][None, :], ln_bias[layer][None, :])
    return h
```

```python
import jax
import jax.numpy as jnp
from jax import lax
from jax.experimental import pallas as pl
from jax.experimental.pallas import tpu as pltpu

D_MODEL = 1024
SB_HEADS = 8
SB_HEAD_DIM = 64
SB_WIDTH = SB_HEADS * SB_HEAD_DIM
CONV_WIDTH = D_MODEL // 2
CONV_KERNEL = 3
DEPTH = 1
DEEPNORM_ALPHA = (2.0 * DEPTH) ** 0.25
LN_EPS = 1e-5

LANES = 128
SUBLANES = 8
HEAD_PAIRS = SB_WIDTH // LANES
BLOCK = 128
UNIT = 64
UNITS_PER_BLOCK = BLOCK // UNIT
WINDOW_BLOCKS = 2
WINDOW_KEYS = WINDOW_BLOCKS * BLOCK
WINDOW_GROUP = 16
IN_ROW_TILE = 1024
OUT_ROW_TILE = 1024
OUT_SUBTILES = 4
VMEM_LIMIT_BYTES = 56 * 1024 * 1024
LOG2_E = 1.4426950408889634
LOG2_SURV_CUTOFF = -104.0 * LOG2_E
MASKED_SCORE = -1e4
LARGE = 1e30

BF16 = jnp.bfloat16
F32 = jnp.float32


def _sigmoid(x):
    return 1.0 / (1.0 + jnp.exp2(x * (-LOG2_E)))


def _silu(x):
    return x * _sigmoid(x)


def _in_proj_kernel(x_ref, w_ref, q_ref, k_ref, v_ref, sa_ref, cu_ref, gz_ref, sga_ref, sgv_ref):
    xb = x_ref[0].astype(BF16)

    def seg(index, width=SB_WIDTH):
        c0 = index * SB_WIDTH
        return jnp.dot(xb, w_ref[:, c0:c0 + width], preferred_element_type=F32)

    def store_pairs(ref, y):
        for p in range(HEAD_PAIRS):
            ref[0, p] = y[:, p * LANES:(p + 1) * LANES].astype(BF16)

    sga_ref[0] = _sigmoid(seg(8, D_MODEL)).astype(BF16)
    store_pairs(q_ref, seg(0) * (SB_HEAD_DIM ** -0.5 * LOG2_E))
    sgv_ref[0] = _sigmoid(seg(10, D_MODEL)).astype(BF16)
    store_pairs(k_ref, seg(1))
    sa_ref[0] = _silu(seg(3)).astype(BF16)
    store_pairs(v_ref, seg(2))
    gate_b = seg(5)
    gz_ref[0] = (gate_b * _silu(seg(7))).astype(BF16)
    u = seg(4)
    cu_ref[0] = (seg(6) * u).astype(BF16)


def _in_proj(x, w_in_bf16):
    b, s, d = x.shape
    tm = IN_ROW_TILE
    grid = (b, s // tm)
    pair_shape = jax.ShapeDtypeStruct((b, HEAD_PAIRS, s, LANES), BF16)
    half_shape = jax.ShapeDtypeStruct((b, s, SB_WIDTH), BF16)
    full_shape = jax.ShapeDtypeStruct((b, s, D_MODEL), BF16)
    pair_spec = pl.BlockSpec((1, HEAD_PAIRS, tm, LANES), lambda bi, ti: (bi, 0, ti, 0))
    half_spec = pl.BlockSpec((1, tm, SB_WIDTH), lambda bi, ti: (bi, ti, 0))
    full_spec = pl.BlockSpec((1, tm, D_MODEL), lambda bi, ti: (bi, ti, 0))
    return pl.pallas_call(
        _in_proj_kernel,
        grid=grid,
        in_specs=[
            pl.BlockSpec((1, tm, d), lambda bi, ti: (bi, ti, 0)),
            pl.BlockSpec(w_in_bf16.shape, lambda bi, ti: (0, 0), pipeline_mode=pl.Buffered(1)),
        ],
        out_specs=[pair_spec] * 3 + [half_spec] * 3 + [full_spec] * 2,
        out_shape=[pair_shape] * 3 + [half_shape] * 3 + [full_shape] * 2,
        compiler_params=pltpu.CompilerParams(
            dimension_semantics=("parallel", "parallel"), vmem_limit_bytes=VMEM_LIMIT_BYTES),
        name="in_proj",
    )(x, w_in_bf16)


def _softplus2(z):
    return jnp.maximum(z, 0.0) + jnp.log2(1.0 + jnp.exp2(-jnp.abs(z)))


def _attn_kernel(q_ref, k_ref, v_ref, uu_ref, o_ref, qs_ref, acc_ref, ls_ref, flag_ref):
    s_len = q_ref.shape[2]
    nb = s_len // BLOCK

    def first_head(rows):
        return lax.broadcasted_iota(jnp.int32, (rows, LANES), 1) < SB_HEAD_DIM

    def offset(index, size):
        if isinstance(index, int):
            return index * size
        return pl.multiple_of(index * size, size)

    def stack_heads(q2):
        zero = jnp.zeros_like(q2)
        keep = first_head(q2.shape[0])
        return jnp.concatenate([jnp.where(keep, q2, zero), jnp.where(keep, zero, q2)], axis=0)

    def unstack_heads(acc):
        rows = acc.shape[0] // 2
        return jnp.where(first_head(rows), acc[:rows], acc[rows:]).astype(o_ref.dtype)

    newest_first = list(reversed(range(WINDOW_BLOCKS)))
    back_units = (WINDOW_KEYS - UNIT) // UNIT
    unit_row = lax.broadcasted_iota(jnp.int32, (2 * UNIT, BLOCK), 0) & (UNIT - 1)
    key_minus_row = lax.broadcasted_iota(jnp.int32, (2 * UNIT, BLOCK), 1) - unit_row

    def window_start(u):
        if isinstance(u, int):
            return max(u - back_units, 0)
        return u - back_units

    def key_bounds(u):
        first = window_start(u) if isinstance(u, int) else None
        bounds = []
        for b in range(WINDOW_BLOCKS):
            lead = (back_units if first is None else u - first) * UNIT - b * BLOCK
            bounds.append(None if lead >= BLOCK else lead)
        return bounds

    def scores(u):
        q2 = q_ref[0, 0, pl.ds(offset(u, UNIT), UNIT), :]
        kw = k_ref[0, 0, pl.ds(offset(window_start(u), UNIT), WINDOW_KEYS), :]
        return lax.dot_general(stack_heads(q2), kw, (((1,), (1,)), ((), ())),
                               preferred_element_type=F32)

    def split_logs(z, bounds):
        sp, log_hit = [], []
        for b, bound in enumerate(bounds):
            zb = z[:, b * BLOCK:(b + 1) * BLOCK]
            if bound is not None:
                zb = jnp.where(key_minus_row < bound, zb, MASKED_SCORE)
            softplus = _softplus2(zb)
            sp.append(softplus.astype(BF16))
            log_hit.append(zb - softplus)
        return jnp.concatenate([sp[b] for b in newest_first], axis=0), log_hit

    def later_sums(sp):
        return jnp.dot(sp, uu_ref[...], preferred_element_type=F32)

    def weights_of(sums, log_hit):
        log_surv = None
        weights = [None] * WINDOW_BLOCKS
        for n, b in enumerate(newest_first):
            block_sums = sums[n * 2 * UNIT:(n + 1) * 2 * UNIT]
            log_a = log_hit[b] + block_sums[:, :BLOCK]
            if log_surv is not None:
                log_a = log_a + log_surv
            weights[b] = jnp.exp2(log_a).astype(BF16)
            total = block_sums[:, BLOCK:]
            log_surv = total if log_surv is None else log_surv + total
        return jnp.concatenate(weights, axis=1), log_surv

    def finish(u, slot, weights, log_surv):
        vw = v_ref[0, 0, pl.ds(offset(window_start(u), UNIT), WINDOW_KEYS), :]
        acc = jnp.dot(weights, vw, preferred_element_type=F32)
        o_ref[0, 0, pl.ds(offset(u, UNIT), UNIT), :] = unstack_heads(acc)
        if isinstance(u, int) and window_start(u) == 0:
            flag = jnp.full((SUBLANES, BLOCK), -LARGE, F32)
        else:
            flag = jnp.max(log_surv.reshape(2 * UNIT // SUBLANES, SUBLANES, BLOCK), axis=0)
        block, half = slot
        flag_ref[block, half * SUBLANES:(half + 1) * SUBLANES, :] = flag

    def window_group(g):
        units = [g * WINDOW_GROUP + n for n in range(WINDOW_GROUP)]
        slots = [(g * (WINDOW_GROUP // UNITS_PER_BLOCK) + n // UNITS_PER_BLOCK, n % UNITS_PER_BLOCK)
                 for n in range(WINDOW_GROUP)]
        zs = [scores(u) for u in units]
        logs = [split_logs(z, key_bounds(u)) for u, z in zip(units, zs)]
        sums = [later_sums(sp) for sp, _ in logs]
        probs = [weights_of(s, log_hit) for s, (_, log_hit) in zip(sums, logs)]
        for u, slot, (w, log_surv) in zip(units, slots, probs):
            finish(u, slot, w, log_surv)

    def traced_group(g, carry):
        window_group(g)
        return carry

    window_group(0)
    lax.fori_loop(1, s_len // (UNIT * WINDOW_GROUP), traced_group, 0)

    block_row = lax.broadcasted_iota(jnp.int32, (2 * BLOCK, BLOCK), 0) & (BLOCK - 1)
    strict = lax.broadcasted_iota(jnp.int32, (2 * BLOCK, BLOCK), 1) < block_row

    def tile(j, masked):
        k2 = k_ref[0, 0, pl.ds(offset(j, BLOCK), BLOCK), :]
        v2 = v_ref[0, 0, pl.ds(offset(j, BLOCK), BLOCK), :]
        z = lax.dot_general(qs_ref[...], k2, (((1,), (1,)), ((), ())),
                            preferred_element_type=F32)
        if masked:
            z = jnp.where(strict, z, MASKED_SCORE)
        softplus = _softplus2(z)
        sums = jnp.dot(softplus.astype(BF16), uu_ref[...], preferred_element_type=F32)
        log_surv = ls_ref[...]
        a = jnp.exp2(z - softplus + sums[:, :BLOCK] + log_surv)
        acc_ref[...] += jnp.dot(a.astype(BF16), v2, preferred_element_type=F32)
        ls_ref[...] = log_surv + sums[:, BLOCK:]

    def general_block(i):
        rows = pl.ds(offset(i, BLOCK), BLOCK)
        qs_ref[...] = stack_heads(q_ref[0, 0, rows, :])
        acc_ref[...] = jnp.zeros_like(acc_ref)
        ls_ref[...] = jnp.zeros_like(ls_ref)
        tile(i, True)

        def more(c):
            j, worst = c
            return jnp.logical_and(j >= 0, worst > LOG2_SURV_CUTOFF)

        def step(c):
            j, _ = c
            tile(j, False)
            return j - 1, jnp.max(ls_ref[...])

        lax.while_loop(more, step, (i - 1, jnp.max(ls_ref[...])))
        o_ref[0, 0, rows, :] = unstack_heads(acc_ref[...])

    @pl.when(jnp.max(flag_ref[...]) > LOG2_SURV_CUTOFF)
    def _():
        def maybe_redo(i, carry):
            @pl.when(jnp.max(flag_ref[i]) > LOG2_SURV_CUTOFF)
            def _():
                general_block(i)
            return carry

        lax.fori_loop(0, nb, maybe_redo, 0)


def _cumsum_weights():
    key = jnp.arange(BLOCK)[:, None]
    col = jnp.arange(2 * BLOCK)[None, :]
    return jnp.where((col >= BLOCK) | (key > col), -1.0, 0.0).astype(BF16)


def _attention(q4, k4, v4):
    b, pairs, s, _ = q4.shape
    assert s % (UNIT * WINDOW_GROUP) == 0 and WINDOW_GROUP % UNITS_PER_BLOCK == 0
    assert WINDOW_GROUP * UNIT >= WINDOW_KEYS
    seq_spec = pl.BlockSpec((1, 1, s, LANES), lambda bi, pi: (bi, pi, 0, 0))
    uu = _cumsum_weights()
    return pl.pallas_call(
        _attn_kernel,
        grid=(b, pairs),
        in_specs=[seq_spec, seq_spec, seq_spec,
                  pl.BlockSpec(uu.shape, lambda bi, pi: (0, 0))],
        out_specs=seq_spec,
        out_shape=jax.ShapeDtypeStruct(q4.shape, BF16),
        scratch_shapes=[
            pltpu.VMEM((2 * BLOCK, LANES), BF16),
            pltpu.VMEM((2 * BLOCK, LANES), F32),
            pltpu.VMEM((2 * BLOCK, BLOCK), F32),
            pltpu.VMEM((s // BLOCK, UNITS_PER_BLOCK * SUBLANES, BLOCK), F32),
        ],
        compiler_params=pltpu.CompilerParams(
            dimension_semantics=("parallel", "parallel"), vmem_limit_bytes=VMEM_LIMIT_BYTES),
        name="stickbreak_attn",
    )(q4, k4, v4, uu)


def _shift_rows(cur, prev_tail, shift):
    rolled = pltpu.roll(cur, shift, 0)
    tail = pltpu.roll(prev_tail, shift, 0)
    row = lax.broadcasted_iota(jnp.int32, tail.shape, 0)
    top = jnp.where(row < shift, tail, rolled[:SUBLANES])
    return jnp.concatenate([top, rolled[SUBLANES:]], axis=0)


def _out_kernel(o_ref, sa_ref, cu_ref, gz_ref, sga_ref, sgv_ref, x_ref, cuh_ref,
                cw_ref, wpa_ref, wpc_ref, wo_ref, gain_ref, bias_ref, out_ref):
    ti = pl.program_id(1)
    rows = out_ref.shape[1] // OUT_SUBTILES
    cw = cw_ref[...]

    def branch_inputs(n):
        lo = n * rows
        o = jnp.concatenate(
            [o_ref[0, p, lo:lo + rows, :] for p in range(HEAD_PAIRS)], axis=1).astype(F32)
        o_attn = o * sa_ref[0, lo:lo + rows, :].astype(F32)
        cu = cu_ref[0, lo:lo + rows, :].astype(F32)
        if n == 0:
            prev = jnp.where(ti > 0, cuh_ref[0].astype(F32), 0.0)
        else:
            prev = cu_ref[0, lo - SUBLANES:lo, :].astype(F32)
        conv = cw[CONV_KERNEL - 1:CONV_KERNEL] * cu
        for tap in range(CONV_KERNEL - 1):
            conv = conv + cw[tap:tap + 1] * _shift_rows(cu, prev, CONV_KERNEL - 1 - tap)
        o_conv = gz_ref[0, lo:lo + rows, :].astype(F32) * conv
        return o_attn.astype(BF16), o_conv.astype(BF16)

    def merge(n, y_attn, y_conv):
        lo = n * rows
        merged = (sga_ref[0, lo:lo + rows, :].astype(F32) * y_attn
                  + sgv_ref[0, lo:lo + rows, :].astype(F32) * y_conv)
        return merged.astype(BF16)

    def finish(n, sub):
        lo = n * rows
        h = DEEPNORM_ALPHA * x_ref[0, lo:lo + rows, :] + sub
        mu = jnp.mean(h, axis=-1, keepdims=True)
        cen = h - mu
        var = jnp.mean(cen * cen, axis=-1, keepdims=True)
        out_ref[0, lo:lo + rows, :] = cen * lax.rsqrt(var + LN_EPS) * gain_ref[...] + bias_ref[...]

    subtiles = range(OUT_SUBTILES)
    branches = [branch_inputs(n) for n in subtiles]
    projected = [(jnp.dot(a, wpa_ref[...], preferred_element_type=F32),
                  jnp.dot(c, wpc_ref[...], preferred_element_type=F32)) for a, c in branches]
    merged = [merge(n, ya, yc) for n, (ya, yc) in zip(subtiles, projected)]
    subs = [jnp.dot(m, wo_ref[...], preferred_element_type=F32) for m in merged]
    for n, sub in zip(subtiles, subs):
        finish(n, sub)


def _out_stage(o4, sa, cu, gz, sga, sgv, x, conv_w, wpa, wpc, wo, gain, bias):
    b, s, d = x.shape
    tm = OUT_ROW_TILE
    grid = (b, s // tm)
    halo_blocks = tm // SUBLANES
    half_spec = pl.BlockSpec((1, tm, SB_WIDTH), lambda bi, ti: (bi, ti, 0))
    full_spec = pl.BlockSpec((1, tm, D_MODEL), lambda bi, ti: (bi, ti, 0))
    halo_spec = pl.BlockSpec(
        (1, SUBLANES, CONV_WIDTH), lambda bi, ti: (bi, jnp.maximum(ti * halo_blocks - 1, 0), 0))

    def whole(a):
        return pl.BlockSpec(a.shape, lambda bi, ti: (0,) * a.ndim)

    return pl.pallas_call(
        _out_kernel,
        grid=grid,
        in_specs=[
            pl.BlockSpec((1, HEAD_PAIRS, tm, LANES), lambda bi, ti: (bi, 0, ti, 0)),
            half_spec, half_spec, half_spec,
            full_spec, full_spec, full_spec,
            halo_spec,
            whole(conv_w), whole(wpa), whole(wpc), whole(wo), whole(gain), whole(bias),
        ],
        out_specs=full_spec,
        out_shape=jax.ShapeDtypeStruct((b, s, d), x.dtype),
        compiler_params=pltpu.CompilerParams(
            dimension_semantics=("parallel", "parallel"), vmem_limit_bytes=VMEM_LIMIT_BYTES),
        name="out_stage",
    )(o4, sa, cu, gz, sga, sgv, x, cu, conv_w, wpa, wpc, wo, gain, bias)


@jax.jit
def kernel(x, w_in, conv_w, w_proj_attn, w_proj_conv, w_out, ln_gain, ln_bias):
    h = x
    for layer in range(DEPTH):
        q4, k4, v4, sa, cu, gz, sga, sgv = _in_proj(h, w_in[layer].astype(BF16))
        o4 = _attention(q4, k4, v4)
        h = _out_stage(
            o4, sa, cu, gz, sga, sgv, h,
            conv_w[layer], w_proj_attn[layer].astype(BF16), w_proj_conv[layer].astype(BF16),
            w_out[layer].astype(BF16), ln_gain[layer][None, :], ln_bias[layer][None, :])
    return h
```

```python
import jax
import jax.numpy as jnp
import numpy as np
from jax import lax
from jax.experimental import pallas as pl
from jax.experimental.pallas import tpu as pltpu

D_MODEL = 1024
SB_HEADS = 8
SB_HEAD_DIM = 64
SB_WIDTH = SB_HEADS * SB_HEAD_DIM
CONV_WIDTH = D_MODEL // 2
CONV_KERNEL = 3
DEPTH = 1
DEEPNORM_ALPHA = (2.0 * DEPTH) ** 0.25
LN_EPS = 1e-5

LANES = 128
SUBLANES = 8
HEAD_PAIRS = SB_WIDTH // LANES
BLOCK = 128
UNIT = 64
UNITS_PER_BLOCK = BLOCK // UNIT
WINDOW_BLOCKS = 2
WINDOW_KEYS = WINDOW_BLOCKS * BLOCK
WINDOW_GROUP = 64
IN_ROW_TILE = 1024
OUT_ROW_TILE = 1024
OUT_SUBTILES = 4
VMEM_LIMIT_BYTES = 56 * 1024 * 1024
LOG2_E = 1.4426950408889634
LOG2_SURV_CUTOFF = -104.0 * LOG2_E
MASKED_SCORE = -1e4
LARGE = 1e30

BF16 = jnp.bfloat16
F32 = jnp.float32


def _sigmoid(x):
    return 0.5 + 0.5 * jnp.tanh(0.5 * x)


def _silu(x):
    return x * _sigmoid(x)


def _in_proj_kernel(x_ref, w_ref, q_ref, k_ref, v_ref, sa_ref, cu_ref, gz_ref, sga_ref, sgv_ref):
    xb = x_ref[0].astype(BF16)

    def seg(index, width=SB_WIDTH):
        c0 = index * SB_WIDTH
        return jnp.dot(xb, w_ref[:, c0:c0 + width], preferred_element_type=F32)

    def store_pairs(ref, y):
        for p in range(HEAD_PAIRS):
            ref[0, p] = y[:, p * LANES:(p + 1) * LANES].astype(BF16)

    sga_ref[0] = _sigmoid(seg(8, D_MODEL)).astype(BF16)
    store_pairs(q_ref, seg(0) * (SB_HEAD_DIM ** -0.5 * LOG2_E))
    sgv_ref[0] = _sigmoid(seg(10, D_MODEL)).astype(BF16)
    store_pairs(k_ref, seg(1))
    sa_ref[0] = _silu(seg(3)).astype(BF16)
    store_pairs(v_ref, seg(2))
    gate_b = seg(5)
    gz_ref[0] = (gate_b * _silu(seg(7))).astype(BF16)
    u = seg(4)
    cu_ref[0] = (seg(6) * u).astype(BF16)


def _in_proj(x, w_in_bf16):
    b, s, d = x.shape
    tm = IN_ROW_TILE
    grid = (b, s // tm)
    pair_shape = jax.ShapeDtypeStruct((b, HEAD_PAIRS, s, LANES), BF16)
    half_shape = jax.ShapeDtypeStruct((b, s, SB_WIDTH), BF16)
    full_shape = jax.ShapeDtypeStruct((b, s, D_MODEL), BF16)
    pair_spec = pl.BlockSpec((1, HEAD_PAIRS, tm, LANES), lambda bi, ti: (bi, 0, ti, 0))
    half_spec = pl.BlockSpec((1, tm, SB_WIDTH), lambda bi, ti: (bi, ti, 0))
    full_spec = pl.BlockSpec((1, tm, D_MODEL), lambda bi, ti: (bi, ti, 0))
    return pl.pallas_call(
        _in_proj_kernel,
        grid=grid,
        in_specs=[
            pl.BlockSpec((1, tm, d), lambda bi, ti: (bi, ti, 0)),
            pl.BlockSpec(w_in_bf16.shape, lambda bi, ti: (0, 0), pipeline_mode=pl.Buffered(1)),
        ],
        out_specs=[pair_spec] * 3 + [half_spec] * 3 + [full_spec] * 2,
        out_shape=[pair_shape] * 3 + [half_shape] * 3 + [full_shape] * 2,
        compiler_params=pltpu.CompilerParams(
            dimension_semantics=("parallel", "parallel"), vmem_limit_bytes=VMEM_LIMIT_BYTES),
        name="in_proj",
    )(x, w_in_bf16)


def _softplus2(z):
    return jnp.maximum(z, 0.0) + jnp.log2(1.0 + jnp.exp2(-jnp.abs(z)))


def _attn_kernel(q_ref, k_ref, v_ref, uu_ref, o_ref, qs_ref, acc_ref, ls_ref, flag_ref):
    s_len = q_ref.shape[2]
    nb = s_len // BLOCK

    def first_head(rows):
        return lax.broadcasted_iota(jnp.int32, (rows, LANES), 1) < SB_HEAD_DIM

    def offset(index, size):
        if isinstance(index, int):
            return index * size
        return pl.multiple_of(index * size, size)

    def stack_heads(q2):
        zero = jnp.zeros_like(q2)
        keep = first_head(q2.shape[0])
        return jnp.concatenate([jnp.where(keep, q2, zero), jnp.where(keep, zero, q2)], axis=0)

    def unstack_heads(acc):
        rows = acc.shape[0] // 2
        return jnp.where(first_head(rows), acc[:rows], acc[rows:]).astype(o_ref.dtype)

    newest_first = list(reversed(range(WINDOW_BLOCKS)))
    back_units = (WINDOW_KEYS - UNIT) // UNIT
    unit_row = lax.broadcasted_iota(jnp.int32, (2 * UNIT, BLOCK), 0) & (UNIT - 1)
    key_minus_row = lax.broadcasted_iota(jnp.int32, (2 * UNIT, BLOCK), 1) - unit_row

    def window_start(u):
        if isinstance(u, int):
            return max(u - back_units, 0)
        return u - back_units

    def key_bounds(u):
        first = window_start(u) if isinstance(u, int) else None
        bounds = []
        for b in range(WINDOW_BLOCKS):
            lead = (back_units if first is None else u - first) * UNIT - b * BLOCK
            bounds.append(None if lead >= BLOCK else lead)
        return bounds

    def scores(u):
        q2 = q_ref[0, 0, pl.ds(offset(u, UNIT), UNIT), :]
        kw = k_ref[0, 0, pl.ds(offset(window_start(u), UNIT), WINDOW_KEYS), :]
        return lax.dot_general(stack_heads(q2), kw, (((1,), (1,)), ((), ())),
                               preferred_element_type=F32)

    def split_logs(z, bounds):
        sp, log_hit = [], []
        for b, bound in enumerate(bounds):
            zb = z[:, b * BLOCK:(b + 1) * BLOCK]
            if bound is not None:
                zb = jnp.where(key_minus_row < bound, zb, MASKED_SCORE)
            softplus = _softplus2(zb)
            sp.append(softplus.astype(BF16))
            log_hit.append(zb - softplus)
        return jnp.concatenate([sp[b] for b in newest_first], axis=0), log_hit

    def later_sums(sp):
        return jnp.dot(sp, uu_ref[...], preferred_element_type=F32)

    def weights_of(sums, log_hit):
        log_surv = None
        weights = [None] * WINDOW_BLOCKS
        for n, b in enumerate(newest_first):
            block_sums = sums[n * 2 * UNIT:(n + 1) * 2 * UNIT]
            log_a = log_hit[b] + block_sums[:, :BLOCK]
            if log_surv is not None:
                log_a = log_a + log_surv
            weights[b] = jnp.exp2(log_a).astype(BF16)
            total = block_sums[:, BLOCK:]
            log_surv = total if log_surv is None else log_surv + total
        return jnp.concatenate(weights, axis=1), log_surv

    def finish(u, slot, weights, log_surv):
        vw = v_ref[0, 0, pl.ds(offset(window_start(u), UNIT), WINDOW_KEYS), :]
        acc = jnp.dot(weights, vw, preferred_element_type=F32)
        o_ref[0, 0, pl.ds(offset(u, UNIT), UNIT), :] = unstack_heads(acc)
        if isinstance(u, int) and window_start(u) == 0:
            flag = jnp.full((SUBLANES, BLOCK), -LARGE, F32)
        else:
            flag = jnp.max(log_surv.reshape(2 * UNIT // SUBLANES, SUBLANES, BLOCK), axis=0)
        block, half = slot
        flag_ref[block, half * SUBLANES:(half + 1) * SUBLANES, :] = flag

    def window_group(g):
        units = [g * WINDOW_GROUP + n for n in range(WINDOW_GROUP)]
        slots = [(g * (WINDOW_GROUP // UNITS_PER_BLOCK) + n // UNITS_PER_BLOCK, n % UNITS_PER_BLOCK)
                 for n in range(WINDOW_GROUP)]
        zs = [scores(u) for u in units]
        logs = [split_logs(z, key_bounds(u)) for u, z in zip(units, zs)]
        sums = [later_sums(sp) for sp, _ in logs]
        probs = [weights_of(s, log_hit) for s, (_, log_hit) in zip(sums, logs)]
        for u, slot, (w, log_surv) in zip(units, slots, probs):
            finish(u, slot, w, log_surv)

    def traced_group(g, carry):
        window_group(g)
        return carry

    window_group(0)
    lax.fori_loop(1, s_len // (UNIT * WINDOW_GROUP), traced_group, 0)

    block_row = lax.broadcasted_iota(jnp.int32, (2 * BLOCK, BLOCK), 0) & (BLOCK - 1)
    strict = lax.broadcasted_iota(jnp.int32, (2 * BLOCK, BLOCK), 1) < block_row

    def tile(j, masked):
        k2 = k_ref[0, 0, pl.ds(offset(j, BLOCK), BLOCK), :]
        v2 = v_ref[0, 0, pl.ds(offset(j, BLOCK), BLOCK), :]
        z = lax.dot_general(qs_ref[...], k2, (((1,), (1,)), ((), ())),
                            preferred_element_type=F32)
        if masked:
            z = jnp.where(strict, z, MASKED_SCORE)
        softplus = _softplus2(z)
        sums = jnp.dot(softplus.astype(BF16), uu_ref[...], preferred_element_type=F32)
        log_surv = ls_ref[...]
        a = jnp.exp2(z - softplus + sums[:, :BLOCK] + log_surv)
        acc_ref[...] += jnp.dot(a.astype(BF16), v2, preferred_element_type=F32)
        ls_ref[...] = log_surv + sums[:, BLOCK:]

    def general_block(i):
        rows = pl.ds(offset(i, BLOCK), BLOCK)
        qs_ref[...] = stack_heads(q_ref[0, 0, rows, :])
        acc_ref[...] = jnp.zeros_like(acc_ref)
        ls_ref[...] = jnp.zeros_like(ls_ref)
        tile(i, True)

        def more(c):
            j, worst = c
            return jnp.logical_and(j >= 0, worst > LOG2_SURV_CUTOFF)

        def step(c):
            j, _ = c
            tile(j, False)
            return j - 1, jnp.max(ls_ref[...])

        lax.while_loop(more, step, (i - 1, jnp.max(ls_ref[...])))
        o_ref[0, 0, rows, :] = unstack_heads(acc_ref[...])

    @pl.when(jnp.max(flag_ref[...]) > LOG2_SURV_CUTOFF)
    def _():
        def maybe_redo(i, carry):
            @pl.when(jnp.max(flag_ref[i]) > LOG2_SURV_CUTOFF)
            def _():
                general_block(i)
            return carry

        lax.fori_loop(0, nb, maybe_redo, 0)


def _cumsum_weights():
    key = np.arange(BLOCK)[:, None]
    col = np.arange(2 * BLOCK)[None, :]
    return jnp.asarray(np.where((col >= BLOCK) | (key > col), -1.0, 0.0), BF16)


def _attention(q4, k4, v4):
    b, pairs, s, _ = q4.shape
    assert s % (UNIT * WINDOW_GROUP) == 0 and WINDOW_GROUP % UNITS_PER_BLOCK == 0
    assert WINDOW_GROUP * UNIT >= WINDOW_KEYS
    seq_spec = pl.BlockSpec((1, 1, s, LANES), lambda bi, pi: (bi, pi, 0, 0))
    uu = _cumsum_weights()
    return pl.pallas_call(
        _attn_kernel,
        grid=(b, pairs),
        in_specs=[seq_spec, seq_spec, seq_spec,
                  pl.BlockSpec(uu.shape, lambda bi, pi: (0, 0))],
        out_specs=seq_spec,
        out_shape=jax.ShapeDtypeStruct(q4.shape, BF16),
        scratch_shapes=[
            pltpu.VMEM((2 * BLOCK, LANES), BF16),
            pltpu.VMEM((2 * BLOCK, LANES), F32),
            pltpu.VMEM((2 * BLOCK, BLOCK), F32),
            pltpu.VMEM((s // BLOCK, UNITS_PER_BLOCK * SUBLANES, BLOCK), F32),
        ],
        compiler_params=pltpu.CompilerParams(
            dimension_semantics=("parallel", "parallel"), vmem_limit_bytes=VMEM_LIMIT_BYTES),
        name="stickbreak_attn",
    )(q4, k4, v4, uu)


def _shift_rows(cur, prev_tail, shift):
    rolled = pltpu.roll(cur, shift, 0)
    tail = pltpu.roll(prev_tail, shift, 0)
    row = lax.broadcasted_iota(jnp.int32, tail.shape, 0)
    top = jnp.where(row < shift, tail, rolled[:SUBLANES])
    return jnp.concatenate([top, rolled[SUBLANES:]], axis=0)


def _out_kernel(o_ref, sa_ref, cu_ref, gz_ref, sga_ref, sgv_ref, x_ref, cuh_ref,
                cw_ref, wpa_ref, wpc_ref, wo_ref, gain_ref, bias_ref, out_ref):
    ti = pl.program_id(1)
    rows = out_ref.shape[1] // OUT_SUBTILES
    cw = cw_ref[...]

    def branch_inputs(n):
        lo = n * rows
        o = jnp.concatenate(
            [o_ref[0, p, lo:lo + rows, :] for p in range(HEAD_PAIRS)], axis=1).astype(F32)
        o_attn = o * sa_ref[0, lo:lo + rows, :].astype(F32)
        cu = cu_ref[0, lo:lo + rows, :].astype(F32)
        if n == 0:
            prev = jnp.where(ti > 0, cuh_ref[0].astype(F32), 0.0)
        else:
            prev = cu_ref[0, lo - SUBLANES:lo, :].astype(F32)
        conv = cw[CONV_KERNEL - 1:CONV_KERNEL] * cu
        for tap in range(CONV_KERNEL - 1):
            conv = conv + cw[tap:tap + 1] * _shift_rows(cu, prev, CONV_KERNEL - 1 - tap)
        o_conv = gz_ref[0, lo:lo + rows, :].astype(F32) * conv
        return o_attn.astype(BF16), o_conv.astype(BF16)

    def merge(n, y_attn, y_conv):
        lo = n * rows
        merged = (sga_ref[0, lo:lo + rows, :].astype(F32) * y_attn
                  + sgv_ref[0, lo:lo + rows, :].astype(F32) * y_conv)
        return merged.astype(BF16)

    def finish(n, sub):
        lo = n * rows
        h = DEEPNORM_ALPHA * x_ref[0, lo:lo + rows, :] + sub
        mu = jnp.mean(h, axis=-1, keepdims=True)
        cen = h - mu
        var = jnp.mean(cen * cen, axis=-1, keepdims=True)
        out_ref[0, lo:lo + rows, :] = cen * lax.rsqrt(var + LN_EPS) * gain_ref[...] + bias_ref[...]

    subtiles = range(OUT_SUBTILES)
    branches = [branch_inputs(n) for n in subtiles]
    wpa = wpa_ref[...].astype(BF16)
    wpc = wpc_ref[...].astype(BF16)
    wo = wo_ref[...].astype(BF16)
    projected = [(jnp.dot(a, wpa, preferred_element_type=F32),
                  jnp.dot(c, wpc, preferred_element_type=F32)) for a, c in branches]
    merged = [merge(n, ya, yc) for n, (ya, yc) in zip(subtiles, projected)]
    subs = [jnp.dot(m, wo, preferred_element_type=F32) for m in merged]
    for n, sub in zip(subtiles, subs):
        finish(n, sub)


def _out_stage(o4, sa, cu, gz, sga, sgv, x, conv_w, wpa, wpc, wo, gain, bias):
    b, s, d = x.shape
    tm = OUT_ROW_TILE
    grid = (b, s // tm)
    halo_blocks = tm // SUBLANES
    half_spec = pl.BlockSpec((1, tm, SB_WIDTH), lambda bi, ti: (bi, ti, 0))
    full_spec = pl.BlockSpec((1, tm, D_MODEL), lambda bi, ti: (bi, ti, 0))
    halo_spec = pl.BlockSpec(
        (1, SUBLANES, CONV_WIDTH), lambda bi, ti: (bi, jnp.maximum(ti * halo_blocks - 1, 0), 0))

    def whole(a):
        return pl.BlockSpec(a.shape, lambda bi, ti: (0,) * a.ndim)

    return pl.pallas_call(
        _out_kernel,
        grid=grid,
        in_specs=[
            pl.BlockSpec((1, HEAD_PAIRS, tm, LANES), lambda bi, ti: (bi, 0, ti, 0)),
            half_spec, half_spec, half_spec,
            full_spec, full_spec, full_spec,
            halo_spec,
            whole(conv_w), whole(wpa), whole(wpc), whole(wo), whole(gain), whole(bias),
        ],
        out_specs=full_spec,
        out_shape=jax.ShapeDtypeStruct((b, s, d), x.dtype),
        compiler_params=pltpu.CompilerParams(
            dimension_semantics=("parallel", "parallel"), vmem_limit_bytes=VMEM_LIMIT_BYTES),
        name="out_stage",
    )(o4, sa, cu, gz, sga, sgv, x, cu, conv_w, wpa, wpc, wo, gain, bias)


@jax.jit
def kernel(x, w_in, conv_w, w_proj_attn, w_proj_conv, w_out, ln_gain, ln_bias):
    h = x
    for layer in range(DEPTH):
        q4, k4, v4, sa, cu, gz, sga, sgv = _in_proj(h, w_in[layer].astype(BF16))
        o4 = _attention(q4, k4, v4)
        h = _out_stage(
            o4, sa, cu, gz, sga, sgv, h,
            conv_w[layer], w_proj_attn[layer], w_proj_conv[layer], w_out[layer], ln_gain[layer][None, :], ln_bias[layer][None, :])
    return h
```

```python
import jax
import jax.numpy as jnp
import numpy as np
from jax import lax
from jax.experimental import pallas as pl
from jax.experimental.pallas import tpu as pltpu

D_MODEL = 1024
SB_HEADS = 8
SB_HEAD_DIM = 64
SB_WIDTH = SB_HEADS * SB_HEAD_DIM
CONV_WIDTH = D_MODEL // 2
CONV_KERNEL = 3
DEPTH = 1
DEEPNORM_ALPHA = (2.0 * DEPTH) ** 0.25
LN_EPS = 1e-5

LANES = 128
SUBLANES = 8
HEAD_PAIRS = SB_WIDTH // LANES
BLOCK = 128
UNIT = 64
UNITS_PER_BLOCK = BLOCK // UNIT
WINDOW_BLOCKS = 2
WINDOW_KEYS = WINDOW_BLOCKS * BLOCK
WINDOW_GROUP = 64
IN_ROW_TILE = 1024
OUT_ROW_TILE = 1024
OUT_SUBTILES = 4
VMEM_LIMIT_BYTES = 56 * 1024 * 1024
LOG2_E = 1.4426950408889634
LOG2_SURV_CUTOFF = -104.0 * LOG2_E
MASKED_SCORE = -1e4
LARGE = 1e30

BF16 = jnp.bfloat16
F32 = jnp.float32


def _sigmoid(x):
    return 0.5 + 0.5 * jnp.tanh(0.5 * x)


def _silu(x):
    return x * _sigmoid(x)


def _shift_rows(cur, prev_tail, shift):
    rolled = pltpu.roll(cur, shift, 0)
    tail = pltpu.roll(prev_tail, shift, 0)
    row = lax.broadcasted_iota(jnp.int32, tail.shape, 0)
    top = jnp.where(row < shift, tail, rolled[:SUBLANES])
    return jnp.concatenate([top, rolled[SUBLANES:]], axis=0)


def _in_proj_kernel(x_ref, w_ref, cw_ref, q_ref, k_ref, v_ref, sa_ref, oc_ref, sga_ref, sgv_ref,
                    carry_ref):
    ti = pl.program_id(1)
    xb = x_ref[0].astype(BF16)

    def seg(index, width=SB_WIDTH):
        c0 = index * SB_WIDTH
        return jnp.dot(xb, w_ref[:, c0:c0 + width], preferred_element_type=F32)

    def store_pairs(ref, y):
        for p in range(HEAD_PAIRS):
            ref[0, p] = y[:, p * LANES:(p + 1) * LANES].astype(BF16)

    cu = seg(6) * seg(4)
    prev = jnp.where(ti > 0, carry_ref[...], 0.0)
    carry_ref[...] = cu[-SUBLANES:]
    cw = cw_ref[...]
    conv = cw[CONV_KERNEL - 1:CONV_KERNEL] * cu
    for tap in range(CONV_KERNEL - 1):
        conv = conv + cw[tap:tap + 1] * _shift_rows(cu, prev, CONV_KERNEL - 1 - tap)
    sga_ref[0] = _sigmoid(seg(8, D_MODEL)).astype(BF16)
    store_pairs(q_ref, seg(0) * (SB_HEAD_DIM ** -0.5 * LOG2_E))
    sgv_ref[0] = _sigmoid(seg(10, D_MODEL)).astype(BF16)
    store_pairs(k_ref, seg(1))
    store_pairs(sa_ref, _silu(seg(3)))
    oc_ref[0] = (seg(5) * _silu(seg(7)) * conv).astype(BF16)
    store_pairs(v_ref, seg(2))


def _in_proj(x, w_in_bf16, conv_w):
    b, s, d = x.shape
    tm = IN_ROW_TILE
    grid = (b, s // tm)
    pair_shape = jax.ShapeDtypeStruct((b, HEAD_PAIRS, s, LANES), BF16)
    half_shape = jax.ShapeDtypeStruct((b, s, SB_WIDTH), BF16)
    full_shape = jax.ShapeDtypeStruct((b, s, D_MODEL), BF16)
    pair_spec = pl.BlockSpec((1, HEAD_PAIRS, tm, LANES), lambda bi, ti: (bi, 0, ti, 0))
    half_spec = pl.BlockSpec((1, tm, SB_WIDTH), lambda bi, ti: (bi, ti, 0))
    full_spec = pl.BlockSpec((1, tm, D_MODEL), lambda bi, ti: (bi, ti, 0))
    return pl.pallas_call(
        _in_proj_kernel,
        grid=grid,
        in_specs=[
            pl.BlockSpec((1, tm, d), lambda bi, ti: (bi, ti, 0)),
            pl.BlockSpec(w_in_bf16.shape, lambda bi, ti: (0, 0), pipeline_mode=pl.Buffered(1)),
            pl.BlockSpec(conv_w.shape, lambda bi, ti: (0, 0)),
        ],
        out_specs=[pair_spec] * 4 + [half_spec] + [full_spec] * 2,
        out_shape=[pair_shape] * 4 + [half_shape] + [full_shape] * 2,
        scratch_shapes=[pltpu.VMEM((SUBLANES, CONV_WIDTH), F32)],
        compiler_params=pltpu.CompilerParams(
            dimension_semantics=("arbitrary", "arbitrary"), vmem_limit_bytes=VMEM_LIMIT_BYTES),
        name="in_proj",
    )(x, w_in_bf16, conv_w)


def _softplus2(z):
    return jnp.maximum(z, 0.0) + jnp.log2(1.0 + jnp.exp2(-jnp.abs(z)))


def _attn_kernel(q_ref, k_ref, v_ref, sa_ref, uu_ref, o_ref, qs_ref, acc_ref, ls_ref, flag_ref):
    s_len = q_ref.shape[2]
    nb = s_len // BLOCK

    def first_head(rows):
        return lax.broadcasted_iota(jnp.int32, (rows, LANES), 1) < SB_HEAD_DIM

    def offset(index, size):
        if isinstance(index, int):
            return index * size
        return pl.multiple_of(index * size, size)

    def stack_heads(q2):
        zero = jnp.zeros_like(q2)
        keep = first_head(q2.shape[0])
        return jnp.concatenate([jnp.where(keep, q2, zero), jnp.where(keep, zero, q2)], axis=0)

    def store_gated(rows, acc):
        n = acc.shape[0] // 2
        o = jnp.where(first_head(n), acc[:n], acc[n:])
        o_ref[0, 0, rows, :] = (o * sa_ref[0, 0, rows, :].astype(F32)).astype(o_ref.dtype)

    newest_first = list(reversed(range(WINDOW_BLOCKS)))
    back_units = (WINDOW_KEYS - UNIT) // UNIT
    unit_row = lax.broadcasted_iota(jnp.int32, (2 * UNIT, BLOCK), 0) & (UNIT - 1)
    key_minus_row = lax.broadcasted_iota(jnp.int32, (2 * UNIT, BLOCK), 1) - unit_row

    def window_start(u):
        if isinstance(u, int):
            return max(u - back_units, 0)
        return u - back_units

    def key_bounds(u):
        first = window_start(u) if isinstance(u, int) else None
        bounds = []
        for b in range(WINDOW_BLOCKS):
            lead = (back_units if first is None else u - first) * UNIT - b * BLOCK
            bounds.append(None if lead >= BLOCK else lead)
        return bounds

    def scores(u):
        q2 = q_ref[0, 0, pl.ds(offset(u, UNIT), UNIT), :]
        kw = k_ref[0, 0, pl.ds(offset(window_start(u), UNIT), WINDOW_KEYS), :]
        return lax.dot_general(stack_heads(q2), kw, (((1,), (1,)), ((), ())),
                               preferred_element_type=F32)

    def split_logs(z, bounds):
        sp, log_hit = [], []
        for b, bound in enumerate(bounds):
            zb = z[:, b * BLOCK:(b + 1) * BLOCK]
            if bound is not None:
                zb = jnp.where(key_minus_row < bound, zb, MASKED_SCORE)
            softplus = _softplus2(zb)
            sp.append(softplus.astype(BF16))
            log_hit.append(zb - softplus)
        return jnp.concatenate([sp[b] for b in newest_first], axis=0), log_hit

    def later_sums(sp):
        return jnp.dot(sp, uu_ref[...], preferred_element_type=F32)

    def weights_of(sums, log_hit):
        log_surv = None
        weights = [None] * WINDOW_BLOCKS
        for n, b in enumerate(newest_first):
            block_sums = sums[n * 2 * UNIT:(n + 1) * 2 * UNIT]
            log_a = log_hit[b] + block_sums[:, :BLOCK]
            if log_surv is not None:
                log_a = log_a + log_surv
            weights[b] = jnp.exp2(log_a).astype(BF16)
            total = block_sums[:, BLOCK:]
            log_surv = total if log_surv is None else log_surv + total
        return jnp.concatenate(weights, axis=1), log_surv

    def finish(u, slot, weights, log_surv):
        vw = v_ref[0, 0, pl.ds(offset(window_start(u), UNIT), WINDOW_KEYS), :]
        acc = jnp.dot(weights, vw, preferred_element_type=F32)
        store_gated(pl.ds(offset(u, UNIT), UNIT), acc)
        if isinstance(u, int) and window_start(u) == 0:
            flag = jnp.full((SUBLANES, BLOCK), -LARGE, F32)
        else:
            flag = jnp.max(log_surv.reshape(2 * UNIT // SUBLANES, SUBLANES, BLOCK), axis=0)
        block, half = slot
        flag_ref[block, half * SUBLANES:(half + 1) * SUBLANES, :] = flag

    def window_group(g):
        units = [g * WINDOW_GROUP + n for n in range(WINDOW_GROUP)]
        slots = [(g * (WINDOW_GROUP // UNITS_PER_BLOCK) + n // UNITS_PER_BLOCK, n % UNITS_PER_BLOCK)
                 for n in range(WINDOW_GROUP)]
        zs = [scores(u) for u in units]
        logs = [split_logs(z, key_bounds(u)) for u, z in zip(units, zs)]
        sums = [later_sums(sp) for sp, _ in logs]
        probs = [weights_of(s, log_hit) for s, (_, log_hit) in zip(sums, logs)]
        for u, slot, (w, log_surv) in zip(units, slots, probs):
            finish(u, slot, w, log_surv)

    def traced_group(g, carry):
        window_group(g)
        return carry

    window_group(0)
    lax.fori_loop(1, s_len // (UNIT * WINDOW_GROUP), traced_group, 0)

    block_row = lax.broadcasted_iota(jnp.int32, (2 * BLOCK, BLOCK), 0) & (BLOCK - 1)
    strict = lax.broadcasted_iota(jnp.int32, (2 * BLOCK, BLOCK), 1) < block_row

    def tile(j, masked):
        k2 = k_ref[0, 0, pl.ds(offset(j, BLOCK), BLOCK), :]
        v2 = v_ref[0, 0, pl.ds(offset(j, BLOCK), BLOCK), :]
        z = lax.dot_general(qs_ref[...], k2, (((1,), (1,)), ((), ())),
                            preferred_element_type=F32)
        if masked:
            z = jnp.where(strict, z, MASKED_SCORE)
        softplus = _softplus2(z)
        sums = jnp.dot(softplus.astype(BF16), uu_ref[...], preferred_element_type=F32)
        log_surv = ls_ref[...]
        a = jnp.exp2(z - softplus + sums[:, :BLOCK] + log_surv)
        acc_ref[...] += jnp.dot(a.astype(BF16), v2, preferred_element_type=F32)
        ls_ref[...] = log_surv + sums[:, BLOCK:]

    def general_block(i):
        rows = pl.ds(offset(i, BLOCK), BLOCK)
        qs_ref[...] = stack_heads(q_ref[0, 0, rows, :])
        acc_ref[...] = jnp.zeros_like(acc_ref)
        ls_ref[...] = jnp.zeros_like(ls_ref)
        tile(i, True)

        def more(c):
            j, worst = c
            return jnp.logical_and(j >= 0, worst > LOG2_SURV_CUTOFF)

        def step(c):
            j, _ = c
            tile(j, False)
            return j - 1, jnp.max(ls_ref[...])

        lax.while_loop(more, step, (i - 1, jnp.max(ls_ref[...])))
        store_gated(rows, acc_ref[...])

    @pl.when(jnp.max(flag_ref[...]) > LOG2_SURV_CUTOFF)
    def _():
        def maybe_redo(i, carry):
            @pl.when(jnp.max(flag_ref[i]) > LOG2_SURV_CUTOFF)
            def _():
                general_block(i)
            return carry

        lax.fori_loop(0, nb, maybe_redo, 0)


def _cumsum_weights():
    key = np.arange(BLOCK)[:, None]
    col = np.arange(2 * BLOCK)[None, :]
    return jnp.asarray(np.where((col >= BLOCK) | (key > col), -1.0, 0.0), BF16)


def _attention(q4, k4, v4, sa4):
    b, pairs, s, _ = q4.shape
    assert s % (UNIT * WINDOW_GROUP) == 0 and WINDOW_GROUP % UNITS_PER_BLOCK == 0
    assert WINDOW_GROUP * UNIT >= WINDOW_KEYS
    seq_spec = pl.BlockSpec((1, 1, s, LANES), lambda bi, pi: (bi, pi, 0, 0))
    uu = _cumsum_weights()
    return pl.pallas_call(
        _attn_kernel,
        grid=(b, pairs),
        in_specs=[seq_spec, seq_spec, seq_spec, seq_spec,
                  pl.BlockSpec(uu.shape, lambda bi, pi: (0, 0))],
        out_specs=seq_spec,
        out_shape=jax.ShapeDtypeStruct(q4.shape, BF16),
        scratch_shapes=[
            pltpu.VMEM((2 * BLOCK, LANES), BF16),
            pltpu.VMEM((2 * BLOCK, LANES), F32),
            pltpu.VMEM((2 * BLOCK, BLOCK), F32),
            pltpu.VMEM((s // BLOCK, UNITS_PER_BLOCK * SUBLANES, BLOCK), F32),
        ],
        compiler_params=pltpu.CompilerParams(
            dimension_semantics=("parallel", "parallel"), vmem_limit_bytes=VMEM_LIMIT_BYTES),
        name="stickbreak_attn",
    )(q4, k4, v4, sa4, uu)


def _out_kernel(oa_ref, oc_ref, sga_ref, sgv_ref, x_ref, wpa_ref, wpc_ref, wo_ref, gain_ref, bias_ref,
                out_ref):
    rows = out_ref.shape[1] // OUT_SUBTILES

    def merge(n, y_attn, y_conv):
        lo = n * rows
        merged = (sga_ref[0, lo:lo + rows, :].astype(F32) * y_attn
                  + sgv_ref[0, lo:lo + rows, :].astype(F32) * y_conv)
        return merged.astype(BF16)

    def finish(n, sub):
        lo = n * rows
        h = DEEPNORM_ALPHA * x_ref[0, lo:lo + rows, :] + sub
        mu = jnp.mean(h, axis=-1, keepdims=True)
        cen = h - mu
        var = jnp.mean(cen * cen, axis=-1, keepdims=True)
        out_ref[0, lo:lo + rows, :] = cen * lax.rsqrt(var + LN_EPS) * gain_ref[...] + bias_ref[...]

    subtiles = range(OUT_SUBTILES)
    wpa = wpa_ref[...].astype(BF16)
    wpc = wpc_ref[...].astype(BF16)
    wo = wo_ref[...].astype(BF16)
    projected = []
    for n in subtiles:
        lo = n * rows
        o_attn = jnp.concatenate(
            [oa_ref[0, p, lo:lo + rows, :] for p in range(HEAD_PAIRS)], axis=1)
        projected.append((jnp.dot(o_attn, wpa, preferred_element_type=F32),
                          jnp.dot(oc_ref[0, lo:lo + rows, :], wpc, preferred_element_type=F32)))
    merged = [merge(n, ya, yc) for n, (ya, yc) in zip(subtiles, projected)]
    subs = [jnp.dot(m, wo, preferred_element_type=F32) for m in merged]
    for n, sub in zip(subtiles, subs):
        finish(n, sub)


def _out_stage(oa4, oc, sga, sgv, x, wpa, wpc, wo, gain, bias):
    b, s, d = x.shape
    tm = OUT_ROW_TILE
    grid = (b, s // tm)
    half_spec = pl.BlockSpec((1, tm, SB_WIDTH), lambda bi, ti: (bi, ti, 0))
    full_spec = pl.BlockSpec((1, tm, D_MODEL), lambda bi, ti: (bi, ti, 0))

    def whole(a):
        return pl.BlockSpec(a.shape, lambda bi, ti: (0,) * a.ndim)

    return pl.pallas_call(
        _out_kernel,
        grid=grid,
        in_specs=[
            pl.BlockSpec((1, HEAD_PAIRS, tm, LANES), lambda bi, ti: (bi, 0, ti, 0)),
            half_spec, full_spec, full_spec, full_spec,
            whole(wpa), whole(wpc), whole(wo), whole(gain), whole(bias),
        ],
        out_specs=full_spec,
        out_shape=jax.ShapeDtypeStruct((b, s, d), x.dtype),
        compiler_params=pltpu.CompilerParams(
            dimension_semantics=("parallel", "parallel"), vmem_limit_bytes=VMEM_LIMIT_BYTES),
        name="out_stage",
    )(oa4, oc, sga, sgv, x, wpa, wpc, wo, gain, bias)


@jax.jit
def kernel(x, w_in, conv_w, w_proj_attn, w_proj_conv, w_out, ln_gain, ln_bias):
    h = x
    for layer in range(DEPTH):
        q4, k4, v4, sa4, oc, sga, sgv = _in_proj(h, w_in[layer].astype(BF16), conv_w[layer])
        oa4 = _attention(q4, k4, v4, sa4)
        h = _out_stage(
            oa4, oc, sga, sgv, h, w_proj_attn[layer], w_proj_conv[layer], w_out[layer],
            ln_gain[layer][None, :], ln_bias[layer][None, :])
    return h
```

```python
import jax
import jax.numpy as jnp
import numpy as np
from jax import lax
from jax.experimental import pallas as pl
from jax.experimental.pallas import tpu as pltpu

D_MODEL = 1024
SB_HEADS = 8
SB_HEAD_DIM = 64
SB_WIDTH = SB_HEADS * SB_HEAD_DIM
CONV_WIDTH = D_MODEL // 2
CONV_KERNEL = 3
DEPTH = 1
DEEPNORM_ALPHA = (2.0 * DEPTH) ** 0.25
LN_EPS = 1e-5

LANES = 128
SUBLANES = 8
HEAD_PAIRS = SB_WIDTH // LANES
BLOCK = 128
UNIT = 64
UNITS_PER_BLOCK = BLOCK // UNIT
WINDOW_BLOCKS = 2
WINDOW_KEYS = WINDOW_BLOCKS * BLOCK
WINDOW_GROUP = 64
IN_ROW_TILE = 1024
OUT_ROW_TILE = 1024
OUT_SUBTILES = 4
VMEM_LIMIT_BYTES = 56 * 1024 * 1024
LOG2_E = 1.4426950408889634
LOG2_SURV_CUTOFF = -104.0 * LOG2_E
MASKED_SCORE = -1e4
LARGE = 1e30

BF16 = jnp.bfloat16
F32 = jnp.float32


def _sigmoid(x):
    return 0.5 + 0.5 * jnp.tanh(0.5 * x)


def _silu(x):
    return x * _sigmoid(x)


def _in_proj_kernel(x_ref, w_ref, q_ref, k_ref, v_ref, sa_ref, cu_ref, gz_ref, sga_ref, sgv_ref):
    xb = x_ref[0].astype(BF16)

    def seg(index, width=SB_WIDTH):
        c0 = index * SB_WIDTH
        return jnp.dot(xb, w_ref[:, c0:c0 + width], preferred_element_type=F32)

    def store_pairs(ref, y):
        for p in range(HEAD_PAIRS):
            ref[0, p] = y[:, p * LANES:(p + 1) * LANES].astype(BF16)

    sga_ref[0] = _sigmoid(seg(8, D_MODEL)).astype(BF16)
    store_pairs(q_ref, seg(0) * (SB_HEAD_DIM ** -0.5 * LOG2_E))
    sgv_ref[0] = _sigmoid(seg(10, D_MODEL)).astype(BF16)
    store_pairs(k_ref, seg(1))
    store_pairs(sa_ref, _silu(seg(3)))
    store_pairs(v_ref, seg(2))
    gate_b = seg(5)
    gz_ref[0] = (gate_b * _silu(seg(7))).astype(BF16)
    u = seg(4)
    cu_ref[0] = (seg(6) * u).astype(BF16)


def _in_proj(x, w_in_bf16):
    b, s, d = x.shape
    tm = IN_ROW_TILE
    grid = (b, s // tm)
    pair_shape = jax.ShapeDtypeStruct((b, HEAD_PAIRS, s, LANES), BF16)
    half_shape = jax.ShapeDtypeStruct((b, s, SB_WIDTH), BF16)
    full_shape = jax.ShapeDtypeStruct((b, s, D_MODEL), BF16)
    pair_spec = pl.BlockSpec((1, HEAD_PAIRS, tm, LANES), lambda bi, ti: (bi, 0, ti, 0))
    half_spec = pl.BlockSpec((1, tm, SB_WIDTH), lambda bi, ti: (bi, ti, 0))
    full_spec = pl.BlockSpec((1, tm, D_MODEL), lambda bi, ti: (bi, ti, 0))
    return pl.pallas_call(
        _in_proj_kernel,
        grid=grid,
        in_specs=[
            pl.BlockSpec((1, tm, d), lambda bi, ti: (bi, ti, 0)),
            pl.BlockSpec(w_in_bf16.shape, lambda bi, ti: (0, 0), pipeline_mode=pl.Buffered(1)),
        ],
        out_specs=[pair_spec] * 4 + [half_spec] * 2 + [full_spec] * 2,
        out_shape=[pair_shape] * 4 + [half_shape] * 2 + [full_shape] * 2,
        compiler_params=pltpu.CompilerParams(
            dimension_semantics=("parallel", "parallel"), vmem_limit_bytes=VMEM_LIMIT_BYTES),
        name="in_proj",
    )(x, w_in_bf16)


def _softplus2(z):
    return jnp.maximum(z, 0.0) + jnp.log2(1.0 + jnp.exp2(-jnp.abs(z)))


def _attn_kernel(q_ref, k_ref, v_ref, sa_ref, uu_ref, o_ref, qs_ref, acc_ref, ls_ref, flag_ref):
    s_len = q_ref.shape[2]
    nb = s_len // BLOCK

    def first_head(rows):
        return lax.broadcasted_iota(jnp.int32, (rows, LANES), 1) < SB_HEAD_DIM

    def offset(index, size):
        if isinstance(index, int):
            return index * size
        return pl.multiple_of(index * size, size)

    def stack_heads(q2):
        zero = jnp.zeros_like(q2)
        keep = first_head(q2.shape[0])
        return jnp.concatenate([jnp.where(keep, q2, zero), jnp.where(keep, zero, q2)], axis=0)

    def store_gated(rows, acc):
        n = acc.shape[0] // 2
        o = jnp.where(first_head(n), acc[:n], acc[n:])
        o_ref[0, 0, rows, :] = (o * sa_ref[0, 0, rows, :].astype(F32)).astype(o_ref.dtype)

    newest_first = list(reversed(range(WINDOW_BLOCKS)))
    back_units = (WINDOW_KEYS - UNIT) // UNIT
    unit_row = lax.broadcasted_iota(jnp.int32, (2 * UNIT, BLOCK), 0) & (UNIT - 1)
    key_minus_row = lax.broadcasted_iota(jnp.int32, (2 * UNIT, BLOCK), 1) - unit_row

    def window_start(u):
        if isinstance(u, int):
            return max(u - back_units, 0)
        return u - back_units

    def key_bounds(u):
        first = window_start(u) if isinstance(u, int) else None
        bounds = []
        for b in range(WINDOW_BLOCKS):
            lead = (back_units if first is None else u - first) * UNIT - b * BLOCK
            bounds.append(None if lead >= BLOCK else lead)
        return bounds

    def scores(u):
        q2 = q_ref[0, 0, pl.ds(offset(u, UNIT), UNIT), :]
        kw = k_ref[0, 0, pl.ds(offset(window_start(u), UNIT), WINDOW_KEYS), :]
        return lax.dot_general(stack_heads(q2), kw, (((1,), (1,)), ((), ())),
                               preferred_element_type=F32)

    def split_logs(z, bounds):
        sp, log_hit = [], []
        for b, bound in enumerate(bounds):
            zb = z[:, b * BLOCK:(b + 1) * BLOCK]
            if bound is not None:
                zb = jnp.where(key_minus_row < bound, zb, MASKED_SCORE)
            softplus = _softplus2(zb)
            sp.append(softplus.astype(BF16))
            log_hit.append(zb - softplus)
        return jnp.concatenate([sp[b] for b in newest_first], axis=0), log_hit

    def later_sums(sp):
        return jnp.dot(sp, uu_ref[...], preferred_element_type=F32)

    def weights_of(sums, log_hit):
        log_surv = None
        weights = [None] * WINDOW_BLOCKS
        for n, b in enumerate(newest_first):
            block_sums = sums[n * 2 * UNIT:(n + 1) * 2 * UNIT]
            log_a = log_hit[b] + block_sums[:, :BLOCK]
            if log_surv is not None:
                log_a = log_a + log_surv
            weights[b] = jnp.exp2(log_a).astype(BF16)
            total = block_sums[:, BLOCK:]
            log_surv = total if log_surv is None else log_surv + total
        return jnp.concatenate(weights, axis=1), log_surv

    def finish(u, slot, weights, log_surv):
        vw = v_ref[0, 0, pl.ds(offset(window_start(u), UNIT), WINDOW_KEYS), :]
        acc = jnp.dot(weights, vw, preferred_element_type=F32)
        store_gated(pl.ds(offset(u, UNIT), UNIT), acc)
        if isinstance(u, int) and window_start(u) == 0:
            flag = jnp.full((SUBLANES, BLOCK), -LARGE, F32)
        else:
            flag = jnp.max(log_surv.reshape(2 * UNIT // SUBLANES, SUBLANES, BLOCK), axis=0)
        block, half = slot
        flag_ref[block, half * SUBLANES:(half + 1) * SUBLANES, :] = flag

    def window_group(g):
        units = [g * WINDOW_GROUP + n for n in range(WINDOW_GROUP)]
        slots = [(g * (WINDOW_GROUP // UNITS_PER_BLOCK) + n // UNITS_PER_BLOCK, n % UNITS_PER_BLOCK)
                 for n in range(WINDOW_GROUP)]
        zs = [scores(u) for u in units]
        logs = [split_logs(z, key_bounds(u)) for u, z in zip(units, zs)]
        sums = [later_sums(sp) for sp, _ in logs]
        probs = [weights_of(s, log_hit) for s, (_, log_hit) in zip(sums, logs)]
        for u, slot, (w, log_surv) in zip(units, slots, probs):
            finish(u, slot, w, log_surv)

    def traced_group(g, carry):
        window_group(g)
        return carry

    window_group(0)
    lax.fori_loop(1, s_len // (UNIT * WINDOW_GROUP), traced_group, 0)

    block_row = lax.broadcasted_iota(jnp.int32, (2 * BLOCK, BLOCK), 0) & (BLOCK - 1)
    strict = lax.broadcasted_iota(jnp.int32, (2 * BLOCK, BLOCK), 1) < block_row

    def tile(j, masked):
        k2 = k_ref[0, 0, pl.ds(offset(j, BLOCK), BLOCK), :]
        v2 = v_ref[0, 0, pl.ds(offset(j, BLOCK), BLOCK), :]
        z = lax.dot_general(qs_ref[...], k2, (((1,), (1,)), ((), ())),
                            preferred_element_type=F32)
        if masked:
            z = jnp.where(strict, z, MASKED_SCORE)
        softplus = _softplus2(z)
        sums = jnp.dot(softplus.astype(BF16), uu_ref[...], preferred_element_type=F32)
        log_surv = ls_ref[...]
        a = jnp.exp2(z - softplus + sums[:, :BLOCK] + log_surv)
        acc_ref[...] += jnp.dot(a.astype(BF16), v2, preferred_element_type=F32)
        ls_ref[...] = log_surv + sums[:, BLOCK:]

    def general_block(i):
        rows = pl.ds(offset(i, BLOCK), BLOCK)
        qs_ref[...] = stack_heads(q_ref[0, 0, rows, :])
        acc_ref[...] = jnp.zeros_like(acc_ref)
        ls_ref[...] = jnp.zeros_like(ls_ref)
        tile(i, True)

        def more(c):
            j, worst = c
            return jnp.logical_and(j >= 0, worst > LOG2_SURV_CUTOFF)

        def step(c):
            j, _ = c
            tile(j, False)
            return j - 1, jnp.max(ls_ref[...])

        lax.while_loop(more, step, (i - 1, jnp.max(ls_ref[...])))
        store_gated(rows, acc_ref[...])

    @pl.when(jnp.max(flag_ref[...]) > LOG2_SURV_CUTOFF)
    def _():
        def maybe_redo(i, carry):
            @pl.when(jnp.max(flag_ref[i]) > LOG2_SURV_CUTOFF)
            def _():
                general_block(i)
            return carry

        lax.fori_loop(0, nb, maybe_redo, 0)


def _cumsum_weights():
    key = np.arange(BLOCK)[:, None]
    col = np.arange(2 * BLOCK)[None, :]
    return jnp.asarray(np.where((col >= BLOCK) | (key > col), -1.0, 0.0), BF16)


def _attention(q4, k4, v4, sa4):
    b, pairs, s, _ = q4.shape
    assert s % (UNIT * WINDOW_GROUP) == 0 and WINDOW_GROUP % UNITS_PER_BLOCK == 0
    assert WINDOW_GROUP * UNIT >= WINDOW_KEYS
    seq_spec = pl.BlockSpec((1, 1, s, LANES), lambda bi, pi: (bi, pi, 0, 0))
    uu = _cumsum_weights()
    return pl.pallas_call(
        _attn_kernel,
        grid=(b, pairs),
        in_specs=[seq_spec, seq_spec, seq_spec, seq_spec,
                  pl.BlockSpec(uu.shape, lambda bi, pi: (0, 0))],
        out_specs=seq_spec,
        out_shape=jax.ShapeDtypeStruct(q4.shape, BF16),
        scratch_shapes=[
            pltpu.VMEM((2 * BLOCK, LANES), BF16),
            pltpu.VMEM((2 * BLOCK, LANES), F32),
            pltpu.VMEM((2 * BLOCK, BLOCK), F32),
            pltpu.VMEM((s // BLOCK, UNITS_PER_BLOCK * SUBLANES, BLOCK), F32),
        ],
        compiler_params=pltpu.CompilerParams(
            dimension_semantics=("parallel", "parallel"), vmem_limit_bytes=VMEM_LIMIT_BYTES),
        name="stickbreak_attn",
    )(q4, k4, v4, sa4, uu)


def _shift_rows(cur, prev_tail, shift):
    rolled = pltpu.roll(cur, shift, 0)
    tail = pltpu.roll(prev_tail, shift, 0)
    row = lax.broadcasted_iota(jnp.int32, tail.shape, 0)
    top = jnp.where(row < shift, tail, rolled[:SUBLANES])
    return jnp.concatenate([top, rolled[SUBLANES:]], axis=0)


def _out_kernel(oa_ref, cu_ref, gz_ref, sga_ref, sgv_ref, x_ref, cuh_ref,
                cw_ref, wpa_ref, wpc_ref, wo_ref, gain_ref, bias_ref, out_ref):
    ti = pl.program_id(1)
    rows = out_ref.shape[1] // OUT_SUBTILES
    cw = cw_ref[...]

    def branch_inputs(n):
        lo = n * rows
        o_attn = jnp.concatenate(
            [oa_ref[0, p, lo:lo + rows, :] for p in range(HEAD_PAIRS)], axis=1)
        cu = cu_ref[0, lo:lo + rows, :].astype(F32)
        if n == 0:
            prev = jnp.where(ti > 0, cuh_ref[0].astype(F32), 0.0)
        else:
            prev = cu_ref[0, lo - SUBLANES:lo, :].astype(F32)
        conv = cw[CONV_KERNEL - 1:CONV_KERNEL] * cu
        for tap in range(CONV_KERNEL - 1):
            conv = conv + cw[tap:tap + 1] * _shift_rows(cu, prev, CONV_KERNEL - 1 - tap)
        o_conv = gz_ref[0, lo:lo + rows, :].astype(F32) * conv
        return o_attn, o_conv.astype(BF16)

    def merge(n, y_attn, y_conv):
        lo = n * rows
        merged = (sga_ref[0, lo:lo + rows, :].astype(F32) * y_attn
                  + sgv_ref[0, lo:lo + rows, :].astype(F32) * y_conv)
        return merged.astype(BF16)

    def finish(n, sub):
        lo = n * rows
        h = DEEPNORM_ALPHA * x_ref[0, lo:lo + rows, :] + sub
        mu = jnp.mean(h, axis=-1, keepdims=True)
        cen = h - mu
        var = jnp.mean(cen * cen, axis=-1, keepdims=True)
        out_ref[0, lo:lo + rows, :] = cen * lax.rsqrt(var + LN_EPS) * gain_ref[...] + bias_ref[...]

    subtiles = range(OUT_SUBTILES)
    branches = [branch_inputs(n) for n in subtiles]
    wpa = wpa_ref[...].astype(BF16)
    wpc = wpc_ref[...].astype(BF16)
    wo = wo_ref[...].astype(BF16)
    projected = [(jnp.dot(a, wpa, preferred_element_type=F32),
                  jnp.dot(c, wpc, preferred_element_type=F32)) for a, c in branches]
    merged = [merge(n, ya, yc) for n, (ya, yc) in zip(subtiles, projected)]
    subs = [jnp.dot(m, wo, preferred_element_type=F32) for m in merged]
    for n, sub in zip(subtiles, subs):
        finish(n, sub)


def _out_stage(oa4, cu, gz, sga, sgv, x, conv_w, wpa, wpc, wo, gain, bias):
    b, s, d = x.shape
    tm = OUT_ROW_TILE
    grid = (b, s // tm)
    halo_blocks = tm // SUBLANES
    half_spec = pl.BlockSpec((1, tm, SB_WIDTH), lambda bi, ti: (bi, ti, 0))
    full_spec = pl.BlockSpec((1, tm, D_MODEL), lambda bi, ti: (bi, ti, 0))
    halo_spec = pl.BlockSpec(
        (1, SUBLANES, CONV_WIDTH), lambda bi, ti: (bi, jnp.maximum(ti * halo_blocks - 1, 0), 0))

    def whole(a):
        return pl.BlockSpec(a.shape, lambda bi, ti: (0,) * a.ndim)

    return pl.pallas_call(
        _out_kernel,
        grid=grid,
        in_specs=[
            pl.BlockSpec((1, HEAD_PAIRS, tm, LANES), lambda bi, ti: (bi, 0, ti, 0)),
            half_spec, half_spec,
            full_spec, full_spec, full_spec,
            halo_spec,
            whole(conv_w), whole(wpa), whole(wpc), whole(wo), whole(gain), whole(bias),
        ],
        out_specs=full_spec,
        out_shape=jax.ShapeDtypeStruct((b, s, d), x.dtype),
        compiler_params=pltpu.CompilerParams(
            dimension_semantics=("parallel", "parallel"), vmem_limit_bytes=VMEM_LIMIT_BYTES),
        name="out_stage",
    )(oa4, cu, gz, sga, sgv, x, cu, conv_w, wpa, wpc, wo, gain, bias)


@jax.jit
def kernel(x, w_in, conv_w, w_proj_attn, w_proj_conv, w_out, ln_gain, ln_bias):
    h = x
    for layer in range(DEPTH):
        q4, k4, v4, sa4, cu, gz, sga, sgv = _in_proj(h, w_in[layer].astype(BF16))
        oa4 = _attention(q4, k4, v4, sa4)
        h = _out_stage(
            oa4, cu, gz, sga, sgv, h,
            conv_w[layer], w_proj_attn[layer], w_proj_conv[layer], w_out[layer],
            ln_gain[layer][None, :], ln_bias[layer][None, :])
    return h
```

```python
import jax
import jax.numpy as jnp
import numpy as np
from jax import lax
from jax.experimental import pallas as pl
from jax.experimental.pallas import tpu as pltpu

D_MODEL = 1024
SB_HEADS = 8
SB_HEAD_DIM = 64
SB_WIDTH = SB_HEADS * SB_HEAD_DIM
CONV_WIDTH = D_MODEL // 2
CONV_KERNEL = 3
DEPTH = 1
DEEPNORM_ALPHA = (2.0 * DEPTH) ** 0.25
LN_EPS = 1e-5

LANES = 128
SUBLANES = 8
HEAD_PAIRS = SB_WIDTH // LANES
BLOCK = 128
UNIT = 64
UNITS_PER_BLOCK = BLOCK // UNIT
WINDOW_BLOCKS = 2
WINDOW_KEYS = WINDOW_BLOCKS * BLOCK
WINDOW_GROUP = 64
IN_ROW_TILE = 1024
W_CHUNK = 512
OUT_ROW_TILE = 1024
OUT_SUBTILES = 4
VMEM_LIMIT_BYTES = 56 * 1024 * 1024
LOG2_E = 1.4426950408889634
LOG2_SURV_CUTOFF = -104.0 * LOG2_E
MASKED_SCORE = -1e4
LARGE = 1e30

BF16 = jnp.bfloat16
F32 = jnp.float32


def _sigmoid(x):
    return 0.5 + 0.5 * jnp.tanh(0.5 * x)


def _silu(x):
    return x * _sigmoid(x)


def _in_proj_kernel(x_ref, w_hbm, q_ref, k_ref, v_ref, sa_ref, cu_ref, gz_ref, sga_ref, sgv_ref,
                    w_ref, stage_ref, sem):
    n_chunks = w_hbm.shape[1] // W_CHUNK

    def chunk_copy(c):
        slot = c % 2
        return pltpu.make_async_copy(
            w_hbm.at[:, pl.ds(c * W_CHUNK, W_CHUNK)], stage_ref.at[slot], sem.at[slot])

    @pl.when(jnp.logical_and(pl.program_id(0) == 0, pl.program_id(1) == 0))
    def _():
        chunk_copy(0).start()
        for c in range(n_chunks):
            if c + 1 < n_chunks:
                chunk_copy(c + 1).start()
            chunk_copy(c).wait()
            w_ref[:, c * W_CHUNK:(c + 1) * W_CHUNK] = stage_ref[c % 2].astype(BF16)

    xb = x_ref[0].astype(BF16)

    def seg(index, width=SB_WIDTH):
        c0 = index * SB_WIDTH
        return jnp.dot(xb, w_ref[:, c0:c0 + width], preferred_element_type=F32)

    def store_pairs(ref, y):
        for p in range(HEAD_PAIRS):
            ref[0, p] = y[:, p * LANES:(p + 1) * LANES].astype(BF16)

    sga_ref[0] = _sigmoid(seg(8, D_MODEL)).astype(BF16)
    store_pairs(q_ref, seg(0) * (SB_HEAD_DIM ** -0.5 * LOG2_E))
    sgv_ref[0] = _sigmoid(seg(10, D_MODEL)).astype(BF16)
    store_pairs(k_ref, seg(1))
    store_pairs(sa_ref, _silu(seg(3)))
    store_pairs(v_ref, seg(2))
    gate_b = seg(5)
    gz_ref[0] = (gate_b * _silu(seg(7))).astype(BF16)
    u = seg(4)
    cu_ref[0] = (seg(6) * u).astype(BF16)


def _in_proj(x, w_in):
    b, s, d = x.shape
    tm = IN_ROW_TILE
    grid = (b, s // tm)
    pair_shape = jax.ShapeDtypeStruct((b, HEAD_PAIRS, s, LANES), BF16)
    half_shape = jax.ShapeDtypeStruct((b, s, SB_WIDTH), BF16)
    full_shape = jax.ShapeDtypeStruct((b, s, D_MODEL), BF16)
    pair_spec = pl.BlockSpec((1, HEAD_PAIRS, tm, LANES), lambda bi, ti: (bi, 0, ti, 0))
    half_spec = pl.BlockSpec((1, tm, SB_WIDTH), lambda bi, ti: (bi, ti, 0))
    full_spec = pl.BlockSpec((1, tm, D_MODEL), lambda bi, ti: (bi, ti, 0))
    return pl.pallas_call(
        _in_proj_kernel,
        grid=grid,
        in_specs=[
            pl.BlockSpec((1, tm, d), lambda bi, ti: (bi, ti, 0)),
            pl.BlockSpec(memory_space=pl.ANY),
        ],
        out_specs=[pair_spec] * 4 + [half_spec] * 2 + [full_spec] * 2,
        out_shape=[pair_shape] * 4 + [half_shape] * 2 + [full_shape] * 2,
        scratch_shapes=[
            pltpu.VMEM(w_in.shape, BF16),
            pltpu.VMEM((2, w_in.shape[0], W_CHUNK), F32),
            pltpu.SemaphoreType.DMA((2,)),
        ],
        compiler_params=pltpu.CompilerParams(
            dimension_semantics=("arbitrary", "arbitrary"), vmem_limit_bytes=VMEM_LIMIT_BYTES),
        name="in_proj",
    )(x, w_in)


def _softplus2(z):
    return jnp.maximum(z, 0.0) + jnp.log2(1.0 + jnp.exp2(-jnp.abs(z)))


def _attn_kernel(q_ref, k_ref, v_ref, sa_ref, uu_ref, o_ref, qs_ref, acc_ref, ls_ref, flag_ref):
    s_len = q_ref.shape[2]
    nb = s_len // BLOCK

    def first_head(rows):
        return lax.broadcasted_iota(jnp.int32, (rows, LANES), 1) < SB_HEAD_DIM

    def offset(index, size):
        if isinstance(index, int):
            return index * size
        return pl.multiple_of(index * size, size)

    def stack_heads(q2):
        zero = jnp.zeros_like(q2)
        keep = first_head(q2.shape[0])
        return jnp.concatenate([jnp.where(keep, q2, zero), jnp.where(keep, zero, q2)], axis=0)

    def store_gated(rows, acc):
        n = acc.shape[0] // 2
        o = jnp.where(first_head(n), acc[:n], acc[n:])
        o_ref[0, 0, rows, :] = (o * sa_ref[0, 0, rows, :].astype(F32)).astype(o_ref.dtype)

    newest_first = list(reversed(range(WINDOW_BLOCKS)))
    back_units = (WINDOW_KEYS - UNIT) // UNIT
    unit_row = lax.broadcasted_iota(jnp.int32, (2 * UNIT, BLOCK), 0) & (UNIT - 1)
    key_minus_row = lax.broadcasted_iota(jnp.int32, (2 * UNIT, BLOCK), 1) - unit_row

    def window_start(u):
        if isinstance(u, int):
            return max(u - back_units, 0)
        return u - back_units

    def key_bounds(u):
        first = window_start(u) if isinstance(u, int) else None
        bounds = []
        for b in range(WINDOW_BLOCKS):
            lead = (back_units if first is None else u - first) * UNIT - b * BLOCK
            bounds.append(None if lead >= BLOCK else lead)
        return bounds

    def scores(u):
        q2 = q_ref[0, 0, pl.ds(offset(u, UNIT), UNIT), :]
        kw = k_ref[0, 0, pl.ds(offset(window_start(u), UNIT), WINDOW_KEYS), :]
        return lax.dot_general(stack_heads(q2), kw, (((1,), (1,)), ((), ())),
                               preferred_element_type=F32)

    def split_logs(z, bounds):
        sp, log_hit = [], []
        for b, bound in enumerate(bounds):
            zb = z[:, b * BLOCK:(b + 1) * BLOCK]
            if bound is not None:
                zb = jnp.where(key_minus_row < bound, zb, MASKED_SCORE)
            softplus = _softplus2(zb)
            sp.append(softplus.astype(BF16))
            log_hit.append(zb - softplus)
        return jnp.concatenate([sp[b] for b in newest_first], axis=0), log_hit

    def later_sums(sp):
        return jnp.dot(sp, uu_ref[...], preferred_element_type=F32)

    def weights_of(sums, log_hit):
        log_surv = None
        weights = [None] * WINDOW_BLOCKS
        for n, b in enumerate(newest_first):
            block_sums = sums[n * 2 * UNIT:(n + 1) * 2 * UNIT]
            log_a = log_hit[b] + block_sums[:, :BLOCK]
            if log_surv is not None:
                log_a = log_a + log_surv
            weights[b] = jnp.exp2(log_a).astype(BF16)
            total = block_sums[:, BLOCK:]
            log_surv = total if log_surv is None else log_surv + total
        return jnp.concatenate(weights, axis=1), log_surv

    def finish(u, slot, weights, log_surv):
        vw = v_ref[0, 0, pl.ds(offset(window_start(u), UNIT), WINDOW_KEYS), :]
        acc = jnp.dot(weights, vw, preferred_element_type=F32)
        store_gated(pl.ds(offset(u, UNIT), UNIT), acc)
        if isinstance(u, int) and window_start(u) == 0:
            flag = jnp.full((SUBLANES, BLOCK), -LARGE, F32)
        else:
            flag = jnp.max(log_surv.reshape(2 * UNIT // SUBLANES, SUBLANES, BLOCK), axis=0)
        block, half = slot
        flag_ref[block, half * SUBLANES:(half + 1) * SUBLANES, :] = flag

    def window_group(g):
        units = [g * WINDOW_GROUP + n for n in range(WINDOW_GROUP)]
        slots = [(g * (WINDOW_GROUP // UNITS_PER_BLOCK) + n // UNITS_PER_BLOCK, n % UNITS_PER_BLOCK)
                 for n in range(WINDOW_GROUP)]
        zs = [scores(u) for u in units]
        logs = [split_logs(z, key_bounds(u)) for u, z in zip(units, zs)]
        sums = [later_sums(sp) for sp, _ in logs]
        probs = [weights_of(s, log_hit) for s, (_, log_hit) in zip(sums, logs)]
        for u, slot, (w, log_surv) in zip(units, slots, probs):
            finish(u, slot, w, log_surv)

    def traced_group(g, carry):
        window_group(g)
        return carry

    window_group(0)
    lax.fori_loop(1, s_len // (UNIT * WINDOW_GROUP), traced_group, 0)

    block_row = lax.broadcasted_iota(jnp.int32, (2 * BLOCK, BLOCK), 0) & (BLOCK - 1)
    strict = lax.broadcasted_iota(jnp.int32, (2 * BLOCK, BLOCK), 1) < block_row

    def tile(j, masked):
        k2 = k_ref[0, 0, pl.ds(offset(j, BLOCK), BLOCK), :]
        v2 = v_ref[0, 0, pl.ds(offset(j, BLOCK), BLOCK), :]
        z = lax.dot_general(qs_ref[...], k2, (((1,), (1,)), ((), ())),
                            preferred_element_type=F32)
        if masked:
            z = jnp.where(strict, z, MASKED_SCORE)
        softplus = _softplus2(z)
        sums = jnp.dot(softplus.astype(BF16), uu_ref[...], preferred_element_type=F32)
        log_surv = ls_ref[...]
        a = jnp.exp2(z - softplus + sums[:, :BLOCK] + log_surv)
        acc_ref[...] += jnp.dot(a.astype(BF16), v2, preferred_element_type=F32)
        ls_ref[...] = log_surv + sums[:, BLOCK:]

    def general_block(i):
        rows = pl.ds(offset(i, BLOCK), BLOCK)
        qs_ref[...] = stack_heads(q_ref[0, 0, rows, :])
        acc_ref[...] = jnp.zeros_like(acc_ref)
        ls_ref[...] = jnp.zeros_like(ls_ref)
        tile(i, True)

        def more(c):
            j, worst = c
            return jnp.logical_and(j >= 0, worst > LOG2_SURV_CUTOFF)

        def step(c):
            j, _ = c
            tile(j, False)
            return j - 1, jnp.max(ls_ref[...])

        lax.while_loop(more, step, (i - 1, jnp.max(ls_ref[...])))
        store_gated(rows, acc_ref[...])

    @pl.when(jnp.max(flag_ref[...]) > LOG2_SURV_CUTOFF)
    def _():
        def maybe_redo(i, carry):
            @pl.when(jnp.max(flag_ref[i]) > LOG2_SURV_CUTOFF)
            def _():
                general_block(i)
            return carry

        lax.fori_loop(0, nb, maybe_redo, 0)


def _cumsum_weights():
    key = np.arange(BLOCK)[:, None]
    col = np.arange(2 * BLOCK)[None, :]
    return jnp.asarray(np.where((col >= BLOCK) | (key > col), -1.0, 0.0), BF16)


def _attention(q4, k4, v4, sa4):
    b, pairs, s, _ = q4.shape
    assert s % (UNIT * WINDOW_GROUP) == 0 and WINDOW_GROUP % UNITS_PER_BLOCK == 0
    assert WINDOW_GROUP * UNIT >= WINDOW_KEYS
    seq_spec = pl.BlockSpec((1, 1, s, LANES), lambda bi, pi: (bi, pi, 0, 0))
    uu = _cumsum_weights()
    return pl.pallas_call(
        _attn_kernel,
        grid=(b, pairs),
        in_specs=[seq_spec, seq_spec, seq_spec, seq_spec,
                  pl.BlockSpec(uu.shape, lambda bi, pi: (0, 0))],
        out_specs=seq_spec,
        out_shape=jax.ShapeDtypeStruct(q4.shape, BF16),
        scratch_shapes=[
            pltpu.VMEM((2 * BLOCK, LANES), BF16),
            pltpu.VMEM((2 * BLOCK, LANES), F32),
            pltpu.VMEM((2 * BLOCK, BLOCK), F32),
            pltpu.VMEM((s // BLOCK, UNITS_PER_BLOCK * SUBLANES, BLOCK), F32),
        ],
        compiler_params=pltpu.CompilerParams(
            dimension_semantics=("parallel", "parallel"), vmem_limit_bytes=VMEM_LIMIT_BYTES),
        name="stickbreak_attn",
    )(q4, k4, v4, sa4, uu)


def _shift_rows(cur, prev_tail, shift):
    rolled = pltpu.roll(cur, shift, 0)
    tail = pltpu.roll(prev_tail, shift, 0)
    row = lax.broadcasted_iota(jnp.int32, tail.shape, 0)
    top = jnp.where(row < shift, tail, rolled[:SUBLANES])
    return jnp.concatenate([top, rolled[SUBLANES:]], axis=0)


def _out_kernel(oa_ref, cu_ref, gz_ref, sga_ref, sgv_ref, x_ref, cuh_ref,
                cw_ref, wpa_ref, wpc_ref, wo_ref, gain_ref, bias_ref, out_ref):
    ti = pl.program_id(1)
    rows = out_ref.shape[1] // OUT_SUBTILES
    cw = cw_ref[...]

    def branch_inputs(n):
        lo = n * rows
        o_attn = jnp.concatenate(
            [oa_ref[0, p, lo:lo + rows, :] for p in range(HEAD_PAIRS)], axis=1)
        cu = cu_ref[0, lo:lo + rows, :].astype(F32)
        if n == 0:
            prev = jnp.where(ti > 0, cuh_ref[0].astype(F32), 0.0)
        else:
            prev = cu_ref[0, lo - SUBLANES:lo, :].astype(F32)
        conv = cw[CONV_KERNEL - 1:CONV_KERNEL] * cu
        for tap in range(CONV_KERNEL - 1):
            conv = conv + cw[tap:tap + 1] * _shift_rows(cu, prev, CONV_KERNEL - 1 - tap)
        o_conv = gz_ref[0, lo:lo + rows, :].astype(F32) * conv
        return o_attn, o_conv.astype(BF16)

    def merge(n, y_attn, y_conv):
        lo = n * rows
        merged = (sga_ref[0, lo:lo + rows, :].astype(F32) * y_attn
                  + sgv_ref[0, lo:lo + rows, :].astype(F32) * y_conv)
        return merged.astype(BF16)

    def finish(n, sub):
        lo = n * rows
        h = DEEPNORM_ALPHA * x_ref[0, lo:lo + rows, :] + sub
        mu = jnp.mean(h, axis=-1, keepdims=True)
        cen = h - mu
        var = jnp.mean(cen * cen, axis=-1, keepdims=True)
        out_ref[0, lo:lo + rows, :] = cen * lax.rsqrt(var + LN_EPS) * gain_ref[...] + bias_ref[...]

    subtiles = range(OUT_SUBTILES)
    branches = [branch_inputs(n) for n in subtiles]
    wpa = wpa_ref[...].astype(BF16)
    wpc = wpc_ref[...].astype(BF16)
    wo = wo_ref[...].astype(BF16)
    projected = [(jnp.dot(a, wpa, preferred_element_type=F32),
                  jnp.dot(c, wpc, preferred_element_type=F32)) for a, c in branches]
    merged = [merge(n, ya, yc) for n, (ya, yc) in zip(subtiles, projected)]
    subs = [jnp.dot(m, wo, preferred_element_type=F32) for m in merged]
    for n, sub in zip(subtiles, subs):
        finish(n, sub)


def _out_stage(oa4, cu, gz, sga, sgv, x, conv_w, wpa, wpc, wo, gain, bias):
    b, s, d = x.shape
    tm = OUT_ROW_TILE
    grid = (b, s // tm)
    halo_blocks = tm // SUBLANES
    half_spec = pl.BlockSpec((1, tm, SB_WIDTH), lambda bi, ti: (bi, ti, 0))
    full_spec = pl.BlockSpec((1, tm, D_MODEL), lambda bi, ti: (bi, ti, 0))
    halo_spec = pl.BlockSpec(
        (1, SUBLANES, CONV_WIDTH), lambda bi, ti: (bi, jnp.maximum(ti * halo_blocks - 1, 0), 0))

    def whole(a):
        return pl.BlockSpec(a.shape, lambda bi, ti: (0,) * a.ndim)

    return pl.pallas_call(
        _out_kernel,
        grid=grid,
        in_specs=[
            pl.BlockSpec((1, HEAD_PAIRS, tm, LANES), lambda bi, ti: (bi, 0, ti, 0)),
            half_spec, half_spec,
            full_spec, full_spec, full_spec,
            halo_spec,
            whole(conv_w), whole(wpa), whole(wpc), whole(wo), whole(gain), whole(bias),
        ],
        out_specs=full_spec,
        out_shape=jax.ShapeDtypeStruct((b, s, d), x.dtype),
        compiler_params=pltpu.CompilerParams(
            dimension_semantics=("parallel", "parallel"), vmem_limit_bytes=VMEM_LIMIT_BYTES),
        name="out_stage",
    )(oa4, cu, gz, sga, sgv, x, cu, conv_w, wpa, wpc, wo, gain, bias)


@jax.jit
def kernel(x, w_in, conv_w, w_proj_attn, w_proj_conv, w_out, ln_gain, ln_bias):
    h = x
    for layer in range(DEPTH):
        q4, k4, v4, sa4, cu, gz, sga, sgv = _in_proj(h, w_in[layer])
        oa4 = _attention(q4, k4, v4, sa4)
        h = _out_stage(
            oa4, cu, gz, sga, sgv, h,
            conv_w[layer], w_proj_attn[layer], w_proj_conv[layer], w_out[layer],
            ln_gain[layer][None, :], ln_bias[layer][None, :])
    return h
```

```python
import jax
import jax.numpy as jnp
import numpy as np
from jax import lax
from jax.experimental import pallas as pl
from jax.experimental.pallas import tpu as pltpu

D_MODEL = 1024
SB_HEADS = 8
SB_HEAD_DIM = 64
SB_WIDTH = SB_HEADS * SB_HEAD_DIM
CONV_WIDTH = D_MODEL // 2
CONV_KERNEL = 3
DEPTH = 1
DEEPNORM_ALPHA = (2.0 * DEPTH) ** 0.25
LN_EPS = 1e-5

LANES = 128
SUBLANES = 8
HEAD_PAIRS = SB_WIDTH // LANES
BLOCK = 128
UNIT = 64
UNITS_PER_BLOCK = BLOCK // UNIT
WINDOW_BLOCKS = 2
WINDOW_KEYS = WINDOW_BLOCKS * BLOCK
WINDOW_GROUP = 64
IN_ROW_TILE = 1024
W_CHUNK = 512
OUT_ROW_TILE = 1024
OUT_SUBTILES = 4
X_RING_SLOTS = 3
VMEM_LIMIT_BYTES = 56 * 1024 * 1024
LOG2_E = 1.4426950408889634
LOG2_SURV_CUTOFF = -104.0 * LOG2_E
MASKED_SCORE = -1e4
LARGE = 1e30

BF16 = jnp.bfloat16
F32 = jnp.float32


def _sigmoid(x):
    return 0.5 + 0.5 * jnp.tanh(0.5 * x)


def _silu(x):
    return x * _sigmoid(x)


def _in_proj_kernel(x_ref, w_hbm, q_ref, k_ref, v_ref, sa_ref, cu_ref, gz_ref, sga_ref, sgv_ref,
                    w_ref, stage_ref, sem):
    n_chunks = w_hbm.shape[1] // W_CHUNK

    def chunk_copy(c):
        slot = c % 2
        return pltpu.make_async_copy(
            w_hbm.at[:, pl.ds(c * W_CHUNK, W_CHUNK)], stage_ref.at[slot], sem.at[slot])

    @pl.when(jnp.logical_and(pl.program_id(0) == 0, pl.program_id(1) == 0))
    def _():
        chunk_copy(0).start()
        for c in range(n_chunks):
            if c + 1 < n_chunks:
                chunk_copy(c + 1).start()
            chunk_copy(c).wait()
            w_ref[:, c * W_CHUNK:(c + 1) * W_CHUNK] = stage_ref[c % 2].astype(BF16)

    xb = x_ref[0].astype(BF16)

    def seg(index, width=SB_WIDTH):
        c0 = index * SB_WIDTH
        return jnp.dot(xb, w_ref[:, c0:c0 + width], preferred_element_type=F32)

    def store_pairs(ref, y):
        for p in range(HEAD_PAIRS):
            ref[0, p] = y[:, p * LANES:(p + 1) * LANES].astype(BF16)

    sga_ref[0] = _sigmoid(seg(8, D_MODEL)).astype(BF16)
    store_pairs(q_ref, seg(0) * (SB_HEAD_DIM ** -0.5 * LOG2_E))
    sgv_ref[0] = _sigmoid(seg(10, D_MODEL)).astype(BF16)
    store_pairs(k_ref, seg(1))
    store_pairs(sa_ref, _silu(seg(3)))
    store_pairs(v_ref, seg(2))
    gate_b = seg(5)
    gz_ref[0] = (gate_b * _silu(seg(7))).astype(BF16)
    u = seg(4)
    cu_ref[0] = (seg(6) * u).astype(BF16)


def _in_proj(x, w_in):
    b, s, d = x.shape
    tm = IN_ROW_TILE
    grid = (b, s // tm)
    pair_shape = jax.ShapeDtypeStruct((b, HEAD_PAIRS, s, LANES), BF16)
    half_shape = jax.ShapeDtypeStruct((b, s, SB_WIDTH), BF16)
    full_shape = jax.ShapeDtypeStruct((b, s, D_MODEL), BF16)
    pair_spec = pl.BlockSpec((1, HEAD_PAIRS, tm, LANES), lambda bi, ti: (bi, 0, ti, 0))
    half_spec = pl.BlockSpec((1, tm, SB_WIDTH), lambda bi, ti: (bi, ti, 0))
    full_spec = pl.BlockSpec((1, tm, D_MODEL), lambda bi, ti: (bi, ti, 0))
    return pl.pallas_call(
        _in_proj_kernel,
        grid=grid,
        in_specs=[
            pl.BlockSpec((1, tm, d), lambda bi, ti: (bi, ti, 0)),
            pl.BlockSpec(memory_space=pl.ANY),
        ],
        out_specs=[pair_spec] * 4 + [half_spec] * 2 + [full_spec] * 2,
        out_shape=[pair_shape] * 4 + [half_shape] * 2 + [full_shape] * 2,
        scratch_shapes=[
            pltpu.VMEM(w_in.shape, BF16),
            pltpu.VMEM((2, w_in.shape[0], W_CHUNK), F32),
            pltpu.SemaphoreType.DMA((2,)),
        ],
        compiler_params=pltpu.CompilerParams(
            dimension_semantics=("arbitrary", "arbitrary"), vmem_limit_bytes=VMEM_LIMIT_BYTES),
        name="in_proj",
    )(x, w_in)


def _softplus2(z):
    return jnp.maximum(z, 0.0) + jnp.log2(1.0 + jnp.exp2(jnp.minimum(z, -z)))


def _attn_kernel(q_ref, k_ref, v_ref, sa_ref, uu_ref, o_ref, qs_ref, acc_ref, ls_ref, flag_ref):
    s_len = q_ref.shape[2]
    nb = s_len // BLOCK

    def first_head(rows):
        return lax.broadcasted_iota(jnp.int32, (rows, LANES), 1) < SB_HEAD_DIM

    def offset(index, size):
        if isinstance(index, int):
            return index * size
        return pl.multiple_of(index * size, size)

    def stack_heads(q2):
        zero = jnp.zeros_like(q2)
        keep = first_head(q2.shape[0])
        return jnp.concatenate([jnp.where(keep, q2, zero), jnp.where(keep, zero, q2)], axis=0)

    def store_gated(rows, acc):
        n = acc.shape[0] // 2
        o = jnp.where(first_head(n), acc[:n], acc[n:])
        o_ref[0, 0, rows, :] = (o * sa_ref[0, 0, rows, :].astype(F32)).astype(o_ref.dtype)

    newest_first = list(reversed(range(WINDOW_BLOCKS)))
    back_units = (WINDOW_KEYS - UNIT) // UNIT
    unit_row = lax.broadcasted_iota(jnp.int32, (2 * UNIT, BLOCK), 0) & (UNIT - 1)
    key_minus_row = lax.broadcasted_iota(jnp.int32, (2 * UNIT, BLOCK), 1) - unit_row

    def window_start(u):
        if isinstance(u, int):
            return max(u - back_units, 0)
        return u - back_units

    def key_bounds(u):
        first = window_start(u) if isinstance(u, int) else None
        bounds = []
        for b in range(WINDOW_BLOCKS):
            lead = (back_units if first is None else u - first) * UNIT - b * BLOCK
            bounds.append(None if lead >= BLOCK else lead)
        return bounds

    def scores(u):
        q2 = q_ref[0, 0, pl.ds(offset(u, UNIT), UNIT), :]
        kw = k_ref[0, 0, pl.ds(offset(window_start(u), UNIT), WINDOW_KEYS), :]
        return lax.dot_general(stack_heads(q2), kw, (((1,), (1,)), ((), ())),
                               preferred_element_type=F32)

    def split_logs(z, bounds):
        sp, log_hit = [], []
        for b, bound in enumerate(bounds):
            zb = z[:, b * BLOCK:(b + 1) * BLOCK]
            if bound is not None:
                zb = jnp.where(key_minus_row < bound, zb, MASKED_SCORE)
            softplus = _softplus2(zb)
            sp.append(softplus.astype(BF16))
            log_hit.append(zb - softplus)
        return jnp.concatenate([sp[b] for b in newest_first], axis=0), log_hit

    def later_sums(sp):
        return jnp.dot(sp, uu_ref[...], preferred_element_type=F32)

    def weights_of(sums, log_hit):
        log_surv = None
        weights = [None] * WINDOW_BLOCKS
        for n, b in enumerate(newest_first):
            block_sums = sums[n * 2 * UNIT:(n + 1) * 2 * UNIT]
            log_a = log_hit[b] + block_sums[:, :BLOCK]
            if log_surv is not None:
                log_a = log_a + log_surv
            weights[b] = jnp.exp2(log_a).astype(BF16)
            total = block_sums[:, BLOCK:]
            log_surv = total if log_surv is None else log_surv + total
        return jnp.concatenate(weights, axis=1), log_surv

    def finish(u, slot, weights, log_surv):
        vw = v_ref[0, 0, pl.ds(offset(window_start(u), UNIT), WINDOW_KEYS), :]
        acc = jnp.dot(weights, vw, preferred_element_type=F32)
        store_gated(pl.ds(offset(u, UNIT), UNIT), acc)
        if isinstance(u, int) and window_start(u) == 0:
            flag = jnp.full((SUBLANES, BLOCK), -LARGE, F32)
        else:
            flag = jnp.max(log_surv.reshape(2 * UNIT // SUBLANES, SUBLANES, BLOCK), axis=0)
        block, half = slot
        flag_ref[block, half * SUBLANES:(half + 1) * SUBLANES, :] = flag

    def window_group(g):
        units = [g * WINDOW_GROUP + n for n in range(WINDOW_GROUP)]
        slots = [(g * (WINDOW_GROUP // UNITS_PER_BLOCK) + n // UNITS_PER_BLOCK, n % UNITS_PER_BLOCK)
                 for n in range(WINDOW_GROUP)]
        zs = [scores(u) for u in units]
        logs = [split_logs(z, key_bounds(u)) for u, z in zip(units, zs)]
        sums = [later_sums(sp) for sp, _ in logs]
        probs = [weights_of(s, log_hit) for s, (_, log_hit) in zip(sums, logs)]
        for u, slot, (w, log_surv) in zip(units, slots, probs):
            finish(u, slot, w, log_surv)

    def traced_group(g, carry):
        window_group(g)
        return carry

    window_group(0)
    lax.fori_loop(1, s_len // (UNIT * WINDOW_GROUP), traced_group, 0)

    block_row = lax.broadcasted_iota(jnp.int32, (2 * BLOCK, BLOCK), 0) & (BLOCK - 1)
    strict = lax.broadcasted_iota(jnp.int32, (2 * BLOCK, BLOCK), 1) < block_row

    def tile(j, masked):
        k2 = k_ref[0, 0, pl.ds(offset(j, BLOCK), BLOCK), :]
        v2 = v_ref[0, 0, pl.ds(offset(j, BLOCK), BLOCK), :]
        z = lax.dot_general(qs_ref[...], k2, (((1,), (1,)), ((), ())),
                            preferred_element_type=F32)
        if masked:
            z = jnp.where(strict, z, MASKED_SCORE)
        softplus = _softplus2(z)
        sums = jnp.dot(softplus.astype(BF16), uu_ref[...], preferred_element_type=F32)
        log_surv = ls_ref[...]
        a = jnp.exp2(z - softplus + sums[:, :BLOCK] + log_surv)
        acc_ref[...] += jnp.dot(a.astype(BF16), v2, preferred_element_type=F32)
        ls_ref[...] = log_surv + sums[:, BLOCK:]

    def general_block(i):
        rows = pl.ds(offset(i, BLOCK), BLOCK)
        qs_ref[...] = stack_heads(q_ref[0, 0, rows, :])
        acc_ref[...] = jnp.zeros_like(acc_ref)
        ls_ref[...] = jnp.zeros_like(ls_ref)
        tile(i, True)

        def more(c):
            j, worst = c
            return jnp.logical_and(j >= 0, worst > LOG2_SURV_CUTOFF)

        def step(c):
            j, _ = c
            tile(j, False)
            return j - 1, jnp.max(ls_ref[...])

        lax.while_loop(more, step, (i - 1, jnp.max(ls_ref[...])))
        store_gated(rows, acc_ref[...])

    @pl.when(jnp.max(flag_ref[...]) > LOG2_SURV_CUTOFF)
    def _():
        def maybe_redo(i, carry):
            @pl.when(jnp.max(flag_ref[i]) > LOG2_SURV_CUTOFF)
            def _():
                general_block(i)
            return carry

        lax.fori_loop(0, nb, maybe_redo, 0)


def _cumsum_weights():
    key = np.arange(BLOCK)[:, None]
    col = np.arange(2 * BLOCK)[None, :]
    return jnp.asarray(np.where((col >= BLOCK) | (key > col), -1.0, 0.0), BF16)


def _attention(q4, k4, v4, sa4):
    b, pairs, s, _ = q4.shape
    assert s % (UNIT * WINDOW_GROUP) == 0 and WINDOW_GROUP % UNITS_PER_BLOCK == 0
    assert WINDOW_GROUP * UNIT >= WINDOW_KEYS
    seq_spec = pl.BlockSpec((1, 1, s, LANES), lambda bi, pi: (bi, pi, 0, 0))
    uu = _cumsum_weights()
    return pl.pallas_call(
        _attn_kernel,
        grid=(b, pairs),
        in_specs=[seq_spec, seq_spec, seq_spec, seq_spec,
                  pl.BlockSpec(uu.shape, lambda bi, pi: (0, 0))],
        out_specs=seq_spec,
        out_shape=jax.ShapeDtypeStruct(q4.shape, BF16),
        scratch_shapes=[
            pltpu.VMEM((2 * BLOCK, LANES), BF16),
            pltpu.VMEM((2 * BLOCK, LANES), F32),
            pltpu.VMEM((2 * BLOCK, BLOCK), F32),
            pltpu.VMEM((s // BLOCK, UNITS_PER_BLOCK * SUBLANES, BLOCK), F32),
        ],
        compiler_params=pltpu.CompilerParams(
            dimension_semantics=("parallel", "parallel"), vmem_limit_bytes=VMEM_LIMIT_BYTES),
        name="stickbreak_attn",
    )(q4, k4, v4, sa4, uu)


def _shift_rows(cur, prev_tail, shift):
    rolled = pltpu.roll(cur, shift, 0)
    tail = pltpu.roll(prev_tail, shift, 0)
    row = lax.broadcasted_iota(jnp.int32, tail.shape, 0)
    top = jnp.where(row < shift, tail, rolled[:SUBLANES])
    return jnp.concatenate([top, rolled[SUBLANES:]], axis=0)


def _out_kernel(oa_ref, cu_ref, gz_ref, sga_ref, sgv_ref, x_hbm, cuh_ref,
                cw_ref, wpa_ref, wpc_ref, wo_ref, gain_ref, bias_ref, out_ref, x_ring, x_sem):
    ti = pl.program_id(1)
    tiles = pl.num_programs(1)
    tm = out_ref.shape[1]
    step = pl.program_id(0) * tiles + ti
    last = pl.num_programs(0) * tiles - 1
    rows = tm // OUT_SUBTILES
    cw = cw_ref[...]

    def x_copy(n):
        slot = n % X_RING_SLOTS
        src = x_hbm.at[n // tiles, pl.ds(pl.multiple_of((n % tiles) * tm, tm), tm), :]
        return pltpu.make_async_copy(src, x_ring.at[slot], x_sem.at[slot])

    @pl.when(step == 0)
    def _():
        for n in range(X_RING_SLOTS - 1):
            x_copy(n).start()

    @pl.when(step + (X_RING_SLOTS - 1) <= last)
    def _():
        x_copy(step + (X_RING_SLOTS - 1)).start()

    x_copy(step).wait()
    x_tile = x_ring.at[step % X_RING_SLOTS]

    def branch_inputs(n):
        lo = n * rows
        o_attn = jnp.concatenate(
            [oa_ref[0, p, lo:lo + rows, :] for p in range(HEAD_PAIRS)], axis=1)
        cu = cu_ref[0, lo:lo + rows, :].astype(F32)
        if n == 0:
            prev = jnp.where(ti > 0, cuh_ref[0].astype(F32), 0.0)
        else:
            prev = cu_ref[0, lo - SUBLANES:lo, :].astype(F32)
        conv = cw[CONV_KERNEL - 1:CONV_KERNEL] * cu
        for tap in range(CONV_KERNEL - 1):
            conv = conv + cw[tap:tap + 1] * _shift_rows(cu, prev, CONV_KERNEL - 1 - tap)
        o_conv = gz_ref[0, lo:lo + rows, :].astype(F32) * conv
        return o_attn, o_conv.astype(BF16)

    def merge(n, y_attn, y_conv):
        lo = n * rows
        merged = (sga_ref[0, lo:lo + rows, :].astype(F32) * y_attn
                  + sgv_ref[0, lo:lo + rows, :].astype(F32) * y_conv)
        return merged.astype(BF16)

    def finish(n, sub):
        lo = n * rows
        h = DEEPNORM_ALPHA * x_tile[lo:lo + rows, :] + sub
        mu = jnp.mean(h, axis=-1, keepdims=True)
        cen = h - mu
        var = jnp.mean(cen * cen, axis=-1, keepdims=True)
        out_ref[0, lo:lo + rows, :] = cen * lax.rsqrt(var + LN_EPS) * gain_ref[...] + bias_ref[...]

    subtiles = range(OUT_SUBTILES)
    branches = [branch_inputs(n) for n in subtiles]
    wpa = wpa_ref[...].astype(BF16)
    wpc = wpc_ref[...].astype(BF16)
    wo = wo_ref[...].astype(BF16)
    projected = [(jnp.dot(a, wpa, preferred_element_type=F32),
                  jnp.dot(c, wpc, preferred_element_type=F32)) for a, c in branches]
    merged = [merge(n, ya, yc) for n, (ya, yc) in zip(subtiles, projected)]
    subs = [jnp.dot(m, wo, preferred_element_type=F32) for m in merged]
    for n, sub in zip(subtiles, subs):
        finish(n, sub)


def _out_stage(oa4, cu, gz, sga, sgv, x, conv_w, wpa, wpc, wo, gain, bias):
    b, s, d = x.shape
    tm = OUT_ROW_TILE
    grid = (b, s // tm)
    halo_blocks = tm // SUBLANES
    half_spec = pl.BlockSpec((1, tm, SB_WIDTH), lambda bi, ti: (bi, ti, 0))
    full_spec = pl.BlockSpec((1, tm, D_MODEL), lambda bi, ti: (bi, ti, 0))
    halo_spec = pl.BlockSpec(
        (1, SUBLANES, CONV_WIDTH), lambda bi, ti: (bi, jnp.maximum(ti * halo_blocks - 1, 0), 0))

    def whole(a):
        return pl.BlockSpec(a.shape, lambda bi, ti: (0,) * a.ndim)

    return pl.pallas_call(
        _out_kernel,
        grid=grid,
        in_specs=[
            pl.BlockSpec((1, HEAD_PAIRS, tm, LANES), lambda bi, ti: (bi, 0, ti, 0)),
            half_spec, half_spec,
            full_spec, full_spec, pl.BlockSpec(memory_space=pl.ANY),
            halo_spec,
            whole(conv_w), whole(wpa), whole(wpc), whole(wo), whole(gain), whole(bias),
        ],
        out_specs=full_spec,
        out_shape=jax.ShapeDtypeStruct((b, s, d), x.dtype),
        scratch_shapes=[
            pltpu.VMEM((X_RING_SLOTS, tm, d), x.dtype),
            pltpu.SemaphoreType.DMA((X_RING_SLOTS,)),
        ],
        compiler_params=pltpu.CompilerParams(
            dimension_semantics=("arbitrary", "arbitrary"), vmem_limit_bytes=VMEM_LIMIT_BYTES),
        name="out_stage",
    )(oa4, cu, gz, sga, sgv, x, cu, conv_w, wpa, wpc, wo, gain, bias)


@jax.jit
def kernel(x, w_in, conv_w, w_proj_attn, w_proj_conv, w_out, ln_gain, ln_bias):
    h = x
    for layer in range(DEPTH):
        q4, k4, v4, sa4, cu, gz, sga, sgv = _in_proj(h, w_in[layer])
        oa4 = _attention(q4, k4, v4, sa4)
        h = _out_stage(
            oa4, cu, gz, sga, sgv, h,
            conv_w[layer], w_proj_attn[layer], w_proj_conv[layer], w_out[layer],
            ln_gain[layer][None, :], ln_bias[layer][None, :])
    return h
```

```python
import jax
import jax.numpy as jnp
import numpy as np
from jax import lax
from jax.experimental import pallas as pl
from jax.experimental.pallas import tpu as pltpu

D_MODEL = 1024
SB_HEADS = 8
SB_HEAD_DIM = 64
SB_WIDTH = SB_HEADS * SB_HEAD_DIM
CONV_WIDTH = D_MODEL // 2
CONV_KERNEL = 3
DEPTH = 1
DEEPNORM_ALPHA = (2.0 * DEPTH) ** 0.25
LN_EPS = 1e-5

LANES = 128
SUBLANES = 8
HEAD_PAIRS = SB_WIDTH // LANES
BLOCK = 128
UNIT = 64
UNITS_PER_BLOCK = BLOCK // UNIT
WINDOW_BLOCKS = 2
WINDOW_KEYS = WINDOW_BLOCKS * BLOCK
WINDOW_GROUP = 64
IN_ROW_TILE = 1024
W_CHUNK = 512
OUT_ROW_TILE = 1024
OUT_SUBTILES = 4
VMEM_LIMIT_BYTES = 56 * 1024 * 1024
LOG2_E = 1.4426950408889634
LOG2_SURV_CUTOFF = -104.0 * LOG2_E
MASKED_SCORE = -1e4
LARGE = 1e30

BF16 = jnp.bfloat16
F32 = jnp.float32


def _sigmoid(x):
    return 0.5 + 0.5 * jnp.tanh(0.5 * x)


def _silu(x):
    return x * _sigmoid(x)


def _in_proj_kernel(x_ref, w_hbm, q_ref, k_ref, v_ref, sa_ref, cu_ref, gz_ref, sga_ref, sgv_ref,
                    w_ref, stage_ref, sem):
    n_chunks = w_hbm.shape[1] // W_CHUNK

    def chunk_copy(c):
        slot = c % 2
        return pltpu.make_async_copy(
            w_hbm.at[:, pl.ds(c * W_CHUNK, W_CHUNK)], stage_ref.at[slot], sem.at[slot])

    @pl.when(jnp.logical_and(pl.program_id(0) == 0, pl.program_id(1) == 0))
    def _():
        chunk_copy(0).start()
        for c in range(n_chunks):
            if c + 1 < n_chunks:
                chunk_copy(c + 1).start()
            chunk_copy(c).wait()
            w_ref[:, c * W_CHUNK:(c + 1) * W_CHUNK] = stage_ref[c % 2].astype(BF16)

    xb = x_ref[0].astype(BF16)

    def seg(index, width=SB_WIDTH):
        c0 = index * SB_WIDTH
        return jnp.dot(xb, w_ref[:, c0:c0 + width], preferred_element_type=F32)

    def store_pairs(ref, y):
        for p in range(HEAD_PAIRS):
            ref[0, p] = y[:, p * LANES:(p + 1) * LANES].astype(BF16)

    sga_ref[0] = _sigmoid(seg(8, D_MODEL)).astype(BF16)
    store_pairs(q_ref, seg(0) * (SB_HEAD_DIM ** -0.5 * LOG2_E))
    sgv_ref[0] = _sigmoid(seg(10, D_MODEL)).astype(BF16)
    store_pairs(k_ref, seg(1))
    store_pairs(sa_ref, _silu(seg(3)))
    store_pairs(v_ref, seg(2))
    gate_b = seg(5)
    gz_ref[0] = (gate_b * _silu(seg(7))).astype(BF16)
    u = seg(4)
    cu_ref[0] = (seg(6) * u).astype(BF16)


def _in_proj(x, w_in):
    b, s, d = x.shape
    tm = IN_ROW_TILE
    grid = (b, s // tm)
    pair_shape = jax.ShapeDtypeStruct((b, HEAD_PAIRS, s, LANES), BF16)
    half_shape = jax.ShapeDtypeStruct((b, s, SB_WIDTH), BF16)
    full_shape = jax.ShapeDtypeStruct((b, s, D_MODEL), BF16)
    pair_spec = pl.BlockSpec((1, HEAD_PAIRS, tm, LANES), lambda bi, ti: (bi, 0, ti, 0))
    half_spec = pl.BlockSpec((1, tm, SB_WIDTH), lambda bi, ti: (bi, ti, 0))
    full_spec = pl.BlockSpec((1, tm, D_MODEL), lambda bi, ti: (bi, ti, 0))
    return pl.pallas_call(
        _in_proj_kernel,
        grid=grid,
        in_specs=[
            pl.BlockSpec((1, tm, d), lambda bi, ti: (bi, ti, 0)),
            pl.BlockSpec(memory_space=pl.ANY),
        ],
        out_specs=[pair_spec] * 4 + [half_spec] * 2 + [full_spec] * 2,
        out_shape=[pair_shape] * 4 + [half_shape] * 2 + [full_shape] * 2,
        scratch_shapes=[
            pltpu.VMEM(w_in.shape, BF16),
            pltpu.VMEM((2, w_in.shape[0], W_CHUNK), F32),
            pltpu.SemaphoreType.DMA((2,)),
        ],
        compiler_params=pltpu.CompilerParams(
            dimension_semantics=("arbitrary", "arbitrary"), vmem_limit_bytes=VMEM_LIMIT_BYTES),
        name="in_proj",
    )(x, w_in)


def _softplus2(z):
    return jnp.maximum(z, 0.0) + jnp.log2(1.0 + jnp.exp2(jnp.minimum(z, -z)))


def _attn_kernel(q_ref, k_ref, v_ref, sa_ref, uu_ref, o_ref, qs_ref, acc_ref, ls_ref, flag_ref):
    s_len = q_ref.shape[2]
    nb = s_len // BLOCK

    def first_head(rows):
        return lax.broadcasted_iota(jnp.int32, (rows, LANES), 1) < SB_HEAD_DIM

    def offset(index, size):
        if isinstance(index, int):
            return index * size
        return pl.multiple_of(index * size, size)

    def stack_heads(q2):
        zero = jnp.zeros_like(q2)
        keep = first_head(q2.shape[0])
        return jnp.concatenate([jnp.where(keep, q2, zero), jnp.where(keep, zero, q2)], axis=0)

    def store_gated(rows, acc):
        n = acc.shape[0] // 2
        o = jnp.where(first_head(n), acc[:n], acc[n:])
        o_ref[0, 0, rows, :] = (o * sa_ref[0, 0, rows, :].astype(F32)).astype(o_ref.dtype)

    newest_first = list(reversed(range(WINDOW_BLOCKS)))
    back_units = (WINDOW_KEYS - UNIT) // UNIT
    unit_row = lax.broadcasted_iota(jnp.int32, (2 * UNIT, BLOCK), 0) & (UNIT - 1)
    key_minus_row = lax.broadcasted_iota(jnp.int32, (2 * UNIT, BLOCK), 1) - unit_row

    def window_start(u):
        if isinstance(u, int):
            return max(u - back_units, 0)
        return u - back_units

    def key_bounds(u):
        first = window_start(u) if isinstance(u, int) else None
        bounds = []
        for b in range(WINDOW_BLOCKS):
            lead = (back_units if first is None else u - first) * UNIT - b * BLOCK
            bounds.append(None if lead >= BLOCK else lead)
        return bounds

    def scores(u):
        q2 = q_ref[0, 0, pl.ds(offset(u, UNIT), UNIT), :]
        kw = k_ref[0, 0, pl.ds(offset(window_start(u), UNIT), WINDOW_KEYS), :]
        return lax.dot_general(stack_heads(q2), kw, (((1,), (1,)), ((), ())),
                               preferred_element_type=F32)

    def split_logs(z, bounds):
        sp, log_hit = [], []
        for b, bound in enumerate(bounds):
            zb = z[:, b * BLOCK:(b + 1) * BLOCK]
            if bound is not None:
                zb = jnp.where(key_minus_row < bound, zb, MASKED_SCORE)
            softplus = _softplus2(zb)
            sp.append(softplus.astype(BF16))
            log_hit.append(zb - softplus)
        return jnp.concatenate([sp[b] for b in newest_first], axis=0), log_hit

    def later_sums(sp):
        return jnp.dot(sp, uu_ref[...], preferred_element_type=F32)

    def weights_of(sums, log_hit):
        log_surv = None
        weights = [None] * WINDOW_BLOCKS
        for n, b in enumerate(newest_first):
            block_sums = sums[n * 2 * UNIT:(n + 1) * 2 * UNIT]
            log_a = log_hit[b] + block_sums[:, :BLOCK]
            if log_surv is not None:
                log_a = log_a + log_surv
            weights[b] = jnp.exp2(log_a).astype(BF16)
            total = block_sums[:, BLOCK:]
            log_surv = total if log_surv is None else log_surv + total
        return jnp.concatenate(weights, axis=1), log_surv

    def finish(u, slot, weights, log_surv):
        vw = v_ref[0, 0, pl.ds(offset(window_start(u), UNIT), WINDOW_KEYS), :]
        acc = jnp.dot(weights, vw, preferred_element_type=F32)
        store_gated(pl.ds(offset(u, UNIT), UNIT), acc)
        if isinstance(u, int) and window_start(u) == 0:
            flag = jnp.full((SUBLANES, BLOCK), -LARGE, F32)
        else:
            flag = jnp.max(log_surv.reshape(2 * UNIT // SUBLANES, SUBLANES, BLOCK), axis=0)
        block, half = slot
        flag_ref[block, half * SUBLANES:(half + 1) * SUBLANES, :] = flag

    def window_group(g):
        units = [g * WINDOW_GROUP + n for n in range(WINDOW_GROUP)]
        slots = [(g * (WINDOW_GROUP // UNITS_PER_BLOCK) + n // UNITS_PER_BLOCK, n % UNITS_PER_BLOCK)
                 for n in range(WINDOW_GROUP)]
        zs = [scores(u) for u in units]
        logs = [split_logs(z, key_bounds(u)) for u, z in zip(units, zs)]
        sums = [later_sums(sp) for sp, _ in logs]
        probs = [weights_of(s, log_hit) for s, (_, log_hit) in zip(sums, logs)]
        for u, slot, (w, log_surv) in zip(units, slots, probs):
            finish(u, slot, w, log_surv)

    def traced_group(g, carry):
        window_group(g)
        return carry

    window_group(0)
    lax.fori_loop(1, s_len // (UNIT * WINDOW_GROUP), traced_group, 0)

    block_row = lax.broadcasted_iota(jnp.int32, (2 * BLOCK, BLOCK), 0) & (BLOCK - 1)
    strict = lax.broadcasted_iota(jnp.int32, (2 * BLOCK, BLOCK), 1) < block_row

    def tile(j, masked):
        k2 = k_ref[0, 0, pl.ds(offset(j, BLOCK), BLOCK), :]
        v2 = v_ref[0, 0, pl.ds(offset(j, BLOCK), BLOCK), :]
        z = lax.dot_general(qs_ref[...], k2, (((1,), (1,)), ((), ())),
                            preferred_element_type=F32)
        if masked:
            z = jnp.where(strict, z, MASKED_SCORE)
        softplus = _softplus2(z)
        sums = jnp.dot(softplus.astype(BF16), uu_ref[...], preferred_element_type=F32)
        log_surv = ls_ref[...]
        a = jnp.exp2(z - softplus + sums[:, :BLOCK] + log_surv)
        acc_ref[...] += jnp.dot(a.astype(BF16), v2, preferred_element_type=F32)
        ls_ref[...] = log_surv + sums[:, BLOCK:]

    def general_block(i):
        rows = pl.ds(offset(i, BLOCK), BLOCK)
        qs_ref[...] = stack_heads(q_ref[0, 0, rows, :])
        acc_ref[...] = jnp.zeros_like(acc_ref)
        ls_ref[...] = jnp.zeros_like(ls_ref)
        tile(i, True)

        def more(c):
            j, worst = c
            return jnp.logical_and(j >= 0, worst > LOG2_SURV_CUTOFF)

        def step(c):
            j, _ = c
            tile(j, False)
            return j - 1, jnp.max(ls_ref[...])

        lax.while_loop(more, step, (i - 1, jnp.max(ls_ref[...])))
        store_gated(rows, acc_ref[...])

    @pl.when(jnp.max(flag_ref[...]) > LOG2_SURV_CUTOFF)
    def _():
        def maybe_redo(i, carry):
            @pl.when(jnp.max(flag_ref[i]) > LOG2_SURV_CUTOFF)
            def _():
                general_block(i)
            return carry

        lax.fori_loop(0, nb, maybe_redo, 0)


def _cumsum_weights():
    key = np.arange(BLOCK)[:, None]
    col = np.arange(2 * BLOCK)[None, :]
    return jnp.asarray(np.where((col >= BLOCK) | (key > col), -1.0, 0.0), BF16)


def _attention(q4, k4, v4, sa4):
    b, pairs, s, _ = q4.shape
    assert s % (UNIT * WINDOW_GROUP) == 0 and WINDOW_GROUP % UNITS_PER_BLOCK == 0
    assert WINDOW_GROUP * UNIT >= WINDOW_KEYS
    seq_spec = pl.BlockSpec((1, 1, s, LANES), lambda bi, pi: (bi, pi, 0, 0))
    uu = _cumsum_weights()
    return pl.pallas_call(
        _attn_kernel,
        grid=(b, pairs),
        in_specs=[seq_spec, seq_spec, seq_spec, seq_spec,
                  pl.BlockSpec(uu.shape, lambda bi, pi: (0, 0))],
        out_specs=seq_spec,
        out_shape=jax.ShapeDtypeStruct(q4.shape, BF16),
        scratch_shapes=[
            pltpu.VMEM((2 * BLOCK, LANES), BF16),
            pltpu.VMEM((2 * BLOCK, LANES), F32),
            pltpu.VMEM((2 * BLOCK, BLOCK), F32),
            pltpu.VMEM((s // BLOCK, UNITS_PER_BLOCK * SUBLANES, BLOCK), F32),
        ],
        compiler_params=pltpu.CompilerParams(
            dimension_semantics=("parallel", "parallel"), vmem_limit_bytes=VMEM_LIMIT_BYTES),
        name="stickbreak_attn",
    )(q4, k4, v4, sa4, uu)


def _shift_rows(cur, prev_tail, shift):
    rolled = pltpu.roll(cur, shift, 0)
    tail = pltpu.roll(prev_tail, shift, 0)
    row = lax.broadcasted_iota(jnp.int32, tail.shape, 0)
    top = jnp.where(row < shift, tail, rolled[:SUBLANES])
    return jnp.concatenate([top, rolled[SUBLANES:]], axis=0)


def _out_kernel(oa_ref, cu_ref, gz_ref, sga_ref, sgv_ref, x_ref, cuh_ref,
                cw_ref, wpa_ref, wpc_ref, wo_ref, gain_ref, bias_ref, out_ref, wpa_bf, wpc_bf, wo_bf):
    ti = pl.program_id(1)

    @pl.when(jnp.logical_and(pl.program_id(0) == 0, ti == 0))
    def _():
        wpa_bf[...] = wpa_ref[...].astype(BF16)
        wpc_bf[...] = wpc_ref[...].astype(BF16)
        wo_bf[...] = wo_ref[...].astype(BF16)

    rows = out_ref.shape[1] // OUT_SUBTILES
    cw = cw_ref[...]

    def branch_inputs(n):
        lo = n * rows
        o_attn = jnp.concatenate(
            [oa_ref[0, p, lo:lo + rows, :] for p in range(HEAD_PAIRS)], axis=1)
        cu = cu_ref[0, lo:lo + rows, :].astype(F32)
        if n == 0:
            prev = jnp.where(ti > 0, cuh_ref[0].astype(F32), 0.0)
        else:
            prev = cu_ref[0, lo - SUBLANES:lo, :].astype(F32)
        conv = cw[CONV_KERNEL - 1:CONV_KERNEL] * cu
        for tap in range(CONV_KERNEL - 1):
            conv = conv + cw[tap:tap + 1] * _shift_rows(cu, prev, CONV_KERNEL - 1 - tap)
        o_conv = gz_ref[0, lo:lo + rows, :].astype(F32) * conv
        return o_attn, o_conv.astype(BF16)

    def merge(n, y_attn, y_conv):
        lo = n * rows
        merged = (sga_ref[0, lo:lo + rows, :].astype(F32) * y_attn
                  + sgv_ref[0, lo:lo + rows, :].astype(F32) * y_conv)
        return merged.astype(BF16)

    def finish(n, sub):
        lo = n * rows
        h = DEEPNORM_ALPHA * x_ref[0, lo:lo + rows, :] + sub
        mu = jnp.mean(h, axis=-1, keepdims=True)
        cen = h - mu
        var = jnp.mean(cen * cen, axis=-1, keepdims=True)
        out_ref[0, lo:lo + rows, :] = cen * lax.rsqrt(var + LN_EPS) * gain_ref[...] + bias_ref[...]

    subtiles = range(OUT_SUBTILES)
    branches = [branch_inputs(n) for n in subtiles]
    wpa = wpa_bf[...]
    wpc = wpc_bf[...]
    wo = wo_bf[...]
    projected = [(jnp.dot(a, wpa, preferred_element_type=F32),
                  jnp.dot(c, wpc, preferred_element_type=F32)) for a, c in branches]
    merged = [merge(n, ya, yc) for n, (ya, yc) in zip(subtiles, projected)]
    subs = [jnp.dot(m, wo, preferred_element_type=F32) for m in merged]
    for n, sub in zip(subtiles, subs):
        finish(n, sub)


def _out_stage(oa4, cu, gz, sga, sgv, x, conv_w, wpa, wpc, wo, gain, bias):
    b, s, d = x.shape
    tm = OUT_ROW_TILE
    grid = (b, s // tm)
    halo_blocks = tm // SUBLANES
    half_spec = pl.BlockSpec((1, tm, SB_WIDTH), lambda bi, ti: (bi, ti, 0))
    full_spec = pl.BlockSpec((1, tm, D_MODEL), lambda bi, ti: (bi, ti, 0))
    halo_spec = pl.BlockSpec(
        (1, SUBLANES, CONV_WIDTH), lambda bi, ti: (bi, jnp.maximum(ti * halo_blocks - 1, 0), 0))

    def whole(a):
        return pl.BlockSpec(a.shape, lambda bi, ti: (0,) * a.ndim)

    return pl.pallas_call(
        _out_kernel,
        grid=grid,
        in_specs=[
            pl.BlockSpec((1, HEAD_PAIRS, tm, LANES), lambda bi, ti: (bi, 0, ti, 0)),
            half_spec, half_spec,
            full_spec, full_spec, full_spec,
            halo_spec,
            whole(conv_w), whole(wpa), whole(wpc), whole(wo), whole(gain), whole(bias),
        ],
        out_specs=full_spec,
        out_shape=jax.ShapeDtypeStruct((b, s, d), x.dtype),
        scratch_shapes=[pltpu.VMEM(wpa.shape, BF16), pltpu.VMEM(wpc.shape, BF16), pltpu.VMEM(wo.shape, BF16)],
        compiler_params=pltpu.CompilerParams(
            dimension_semantics=("arbitrary", "arbitrary"), vmem_limit_bytes=VMEM_LIMIT_BYTES),
        name="out_stage",
    )(oa4, cu, gz, sga, sgv, x, cu, conv_w, wpa, wpc, wo, gain, bias)


@jax.jit
def kernel(x, w_in, conv_w, w_proj_attn, w_proj_conv, w_out, ln_gain, ln_bias):
    h = x
    for layer in range(DEPTH):
        q4, k4, v4, sa4, cu, gz, sga, sgv = _in_proj(h, w_in[layer])
        oa4 = _attention(q4, k4, v4, sa4)
        h = _out_stage(
            oa4, cu, gz, sga, sgv, h,
            conv_w[layer], w_proj_attn[layer], w_proj_conv[layer], w_out[layer],
            ln_gain[layer][None, :], ln_bias[layer][None, :])
    return h
```

```python
import jax
import jax.numpy as jnp
import numpy as np
from jax import lax
from jax.experimental import pallas as pl
from jax.experimental.pallas import tpu as pltpu

D_MODEL = 1024
SB_HEADS = 8
SB_HEAD_DIM = 64
SB_WIDTH = SB_HEADS * SB_HEAD_DIM
CONV_WIDTH = D_MODEL // 2
CONV_KERNEL = 3
DEPTH = 1
DEEPNORM_ALPHA = (2.0 * DEPTH) ** 0.25
LN_EPS = 1e-5

LANES = 128
SUBLANES = 8
HEAD_PAIRS = SB_WIDTH // LANES
BLOCK = 128
UNIT = 64
UNITS_PER_BLOCK = BLOCK // UNIT
WINDOW_BLOCKS = 2
WINDOW_KEYS = WINDOW_BLOCKS * BLOCK
WINDOW_GROUP = 64
IN_ROW_TILE = 1024
W_CHUNK = SB_WIDTH
SEGMENT_ORDER = (8, 9, 0, 10, 11, 1, 3, 2, 5, 7, 4, 6)
OUT_ROW_TILE = 1024
OUT_SUBTILES = 4
VMEM_LIMIT_BYTES = 56 * 1024 * 1024
LOG2_E = 1.4426950408889634
LOG2_SURV_CUTOFF = -104.0 * LOG2_E
MASKED_SCORE = -1e4
LARGE = 1e30

BF16 = jnp.bfloat16
F32 = jnp.float32


def _sigmoid(x):
    return 0.5 + 0.5 * jnp.tanh(0.5 * x)


def _silu(x):
    return x * _sigmoid(x)


def _in_proj_kernel(x_ref, w_hbm, q_ref, k_ref, v_ref, sa_ref, cu_ref, gz_ref, sga_ref, sgv_ref,
                    w_ref, stage_ref, sem):
    xb = x_ref[0].astype(BF16)

    def chunk_copy(position):
        c, slot = SEGMENT_ORDER[position], position % 2
        return pltpu.make_async_copy(
            w_hbm.at[:, pl.ds(c * W_CHUNK, W_CHUNK)], stage_ref.at[slot], sem.at[slot])

    def store_pairs(ref, y):
        for p in range(HEAD_PAIRS):
            ref[0, p] = y[:, p * LANES:(p + 1) * LANES].astype(BF16)

    def project(staging):
        requested = []

        def seg(index, width=SB_WIDTH):
            for c in range(index, index + width // W_CHUNK):
                position = len(requested)
                assert SEGMENT_ORDER[position] == c
                requested.append(c)
                if staging:
                    if position + 1 < len(SEGMENT_ORDER):
                        chunk_copy(position + 1).start()
                    chunk_copy(position).wait()
                    w_ref[:, c * W_CHUNK:(c + 1) * W_CHUNK] = stage_ref[position % 2].astype(BF16)
            c0 = index * W_CHUNK
            return jnp.dot(xb, w_ref[:, c0:c0 + width], preferred_element_type=F32)

        sga_ref[0] = _sigmoid(seg(8, D_MODEL)).astype(BF16)
        store_pairs(q_ref, seg(0) * (SB_HEAD_DIM ** -0.5 * LOG2_E))
        sgv_ref[0] = _sigmoid(seg(10, D_MODEL)).astype(BF16)
        store_pairs(k_ref, seg(1))
        store_pairs(sa_ref, _silu(seg(3)))
        store_pairs(v_ref, seg(2))
        gate_b = seg(5)
        gz_ref[0] = (gate_b * _silu(seg(7))).astype(BF16)
        u = seg(4)
        cu_ref[0] = (seg(6) * u).astype(BF16)
        assert len(requested) == len(SEGMENT_ORDER)

    first_step = jnp.logical_and(pl.program_id(0) == 0, pl.program_id(1) == 0)

    @pl.when(first_step)
    def _():
        chunk_copy(0).start()
        project(staging=True)

    @pl.when(jnp.logical_not(first_step))
    def _():
        project(staging=False)


def _in_proj(x, w_in):
    b, s, d = x.shape
    tm = IN_ROW_TILE
    grid = (b, s // tm)
    pair_shape = jax.ShapeDtypeStruct((b, HEAD_PAIRS, s, LANES), BF16)
    half_shape = jax.ShapeDtypeStruct((b, s, SB_WIDTH), BF16)
    full_shape = jax.ShapeDtypeStruct((b, s, D_MODEL), BF16)
    pair_spec = pl.BlockSpec((1, HEAD_PAIRS, tm, LANES), lambda bi, ti: (bi, 0, ti, 0))
    half_spec = pl.BlockSpec((1, tm, SB_WIDTH), lambda bi, ti: (bi, ti, 0))
    full_spec = pl.BlockSpec((1, tm, D_MODEL), lambda bi, ti: (bi, ti, 0))
    return pl.pallas_call(
        _in_proj_kernel,
        grid=grid,
        in_specs=[
            pl.BlockSpec((1, tm, d), lambda bi, ti: (bi, ti, 0)),
            pl.BlockSpec(memory_space=pl.ANY),
        ],
        out_specs=[pair_spec] * 4 + [half_spec] * 2 + [full_spec] * 2,
        out_shape=[pair_shape] * 4 + [half_shape] * 2 + [full_shape] * 2,
        scratch_shapes=[
            pltpu.VMEM(w_in.shape, BF16),
            pltpu.VMEM((2, w_in.shape[0], W_CHUNK), F32),
            pltpu.SemaphoreType.DMA((2,)),
        ],
        compiler_params=pltpu.CompilerParams(
            dimension_semantics=("arbitrary", "arbitrary"), vmem_limit_bytes=VMEM_LIMIT_BYTES),
        name="in_proj",
    )(x, w_in)


def _softplus2(z):
    return jnp.maximum(z, 0.0) + jnp.log2(1.0 + jnp.exp2(jnp.minimum(z, -z)))


def _attn_kernel(q_ref, k_ref, v_ref, sa_ref, uu_ref, o_ref, qs_ref, acc_ref, ls_ref, flag_ref):
    s_len = q_ref.shape[2]
    nb = s_len // BLOCK

    def first_head(rows):
        return lax.broadcasted_iota(jnp.int32, (rows, LANES), 1) < SB_HEAD_DIM

    def offset(index, size):
        if isinstance(index, int):
            return index * size
        return pl.multiple_of(index * size, size)

    def stack_heads(q2):
        zero = jnp.zeros_like(q2)
        keep = first_head(q2.shape[0])
        return jnp.concatenate([jnp.where(keep, q2, zero), jnp.where(keep, zero, q2)], axis=0)

    def store_gated(rows, acc):
        n = acc.shape[0] // 2
        o = jnp.where(first_head(n), acc[:n], acc[n:])
        o_ref[0, 0, rows, :] = (o * sa_ref[0, 0, rows, :].astype(F32)).astype(o_ref.dtype)

    newest_first = list(reversed(range(WINDOW_BLOCKS)))
    back_units = (WINDOW_KEYS - UNIT) // UNIT
    unit_row = lax.broadcasted_iota(jnp.int32, (2 * UNIT, BLOCK), 0) & (UNIT - 1)
    key_minus_row = lax.broadcasted_iota(jnp.int32, (2 * UNIT, BLOCK), 1) - unit_row

    def window_start(u):
        if isinstance(u, int):
            return max(u - back_units, 0)
        return u - back_units

    def key_bounds(u):
        first = window_start(u) if isinstance(u, int) else None
        bounds = []
        for b in range(WINDOW_BLOCKS):
            lead = (back_units if first is None else u - first) * UNIT - b * BLOCK
            bounds.append(None if lead >= BLOCK else lead)
        return bounds

    def scores(u):
        q2 = q_ref[0, 0, pl.ds(offset(u, UNIT), UNIT), :]
        kw = k_ref[0, 0, pl.ds(offset(window_start(u), UNIT), WINDOW_KEYS), :]
        return lax.dot_general(stack_heads(q2), kw, (((1,), (1,)), ((), ())),
                               preferred_element_type=F32)

    def split_logs(z, bounds):
        sp, log_hit = [], []
        for b, bound in enumerate(bounds):
            zb = z[:, b * BLOCK:(b + 1) * BLOCK]
            if bound is not None:
                zb = jnp.where(key_minus_row < bound, zb, MASKED_SCORE)
            softplus = _softplus2(zb)
            sp.append(softplus.astype(BF16))
            log_hit.append(zb - softplus)
        return jnp.concatenate([sp[b] for b in newest_first], axis=0), log_hit

    def later_sums(sp):
        return jnp.dot(sp, uu_ref[...], preferred_element_type=F32)

    def weights_of(sums, log_hit):
        log_surv = None
        weights = [None] * WINDOW_BLOCKS
        for n, b in enumerate(newest_first):
            block_sums = sums[n * 2 * UNIT:(n + 1) * 2 * UNIT]
            log_a = log_hit[b] + block_sums[:, :BLOCK]
            if log_surv is not None:
                log_a = log_a + log_surv
            weights[b] = jnp.exp2(log_a).astype(BF16)
            total = block_sums[:, BLOCK:]
            log_surv = total if log_surv is None else log_surv + total
        return jnp.concatenate(weights, axis=1), log_surv

    def finish(u, slot, weights, log_surv):
        vw = v_ref[0, 0, pl.ds(offset(window_start(u), UNIT), WINDOW_KEYS), :]
        acc = jnp.dot(weights, vw, preferred_element_type=F32)
        store_gated(pl.ds(offset(u, UNIT), UNIT), acc)
        if isinstance(u, int) and window_start(u) == 0:
            flag = jnp.full((SUBLANES, BLOCK), -LARGE, F32)
        else:
            flag = jnp.max(log_surv.reshape(2 * UNIT // SUBLANES, SUBLANES, BLOCK), axis=0)
        block, half = slot
        flag_ref[block, half * SUBLANES:(half + 1) * SUBLANES, :] = flag

    def window_group(g):
        units = [g * WINDOW_GROUP + n for n in range(WINDOW_GROUP)]
        slots = [(g * (WINDOW_GROUP // UNITS_PER_BLOCK) + n // UNITS_PER_BLOCK, n % UNITS_PER_BLOCK)
                 for n in range(WINDOW_GROUP)]
        zs = [scores(u) for u in units]
        logs = [split_logs(z, key_bounds(u)) for u, z in zip(units, zs)]
        sums = [later_sums(sp) for sp, _ in logs]
        probs = [weights_of(s, log_hit) for s, (_, log_hit) in zip(sums, logs)]
        for u, slot, (w, log_surv) in zip(units, slots, probs):
            finish(u, slot, w, log_surv)

    def traced_group(g, carry):
        window_group(g)
        return carry

    window_group(0)
    lax.fori_loop(1, s_len // (UNIT * WINDOW_GROUP), traced_group, 0)

    block_row = lax.broadcasted_iota(jnp.int32, (2 * BLOCK, BLOCK), 0) & (BLOCK - 1)
    strict = lax.broadcasted_iota(jnp.int32, (2 * BLOCK, BLOCK), 1) < block_row

    def tile(j, masked):
        k2 = k_ref[0, 0, pl.ds(offset(j, BLOCK), BLOCK), :]
        v2 = v_ref[0, 0, pl.ds(offset(j, BLOCK), BLOCK), :]
        z = lax.dot_general(qs_ref[...], k2, (((1,), (1,)), ((), ())),
                            preferred_element_type=F32)
        if masked:
            z = jnp.where(strict, z, MASKED_SCORE)
        softplus = _softplus2(z)
        sums = jnp.dot(softplus.astype(BF16), uu_ref[...], preferred_element_type=F32)
        log_surv = ls_ref[...]
        a = jnp.exp2(z - softplus + sums[:, :BLOCK] + log_surv)
        acc_ref[...] += jnp.dot(a.astype(BF16), v2, preferred_element_type=F32)
        ls_ref[...] = log_surv + sums[:, BLOCK:]

    def general_block(i):
        rows = pl.ds(offset(i, BLOCK), BLOCK)
        qs_ref[...] = stack_heads(q_ref[0, 0, rows, :])
        acc_ref[...] = jnp.zeros_like(acc_ref)
        ls_ref[...] = jnp.zeros_like(ls_ref)
        tile(i, True)

        def more(c):
            j, worst = c
            return jnp.logical_and(j >= 0, worst > LOG2_SURV_CUTOFF)

        def step(c):
            j, _ = c
            tile(j, False)
            return j - 1, jnp.max(ls_ref[...])

        lax.while_loop(more, step, (i - 1, jnp.max(ls_ref[...])))
        store_gated(rows, acc_ref[...])

    @pl.when(jnp.max(flag_ref[...]) > LOG2_SURV_CUTOFF)
    def _():
        def maybe_redo(i, carry):
            @pl.when(jnp.max(flag_ref[i]) > LOG2_SURV_CUTOFF)
            def _():
                general_block(i)
            return carry

        lax.fori_loop(0, nb, maybe_redo, 0)


def _cumsum_weights():
    key = np.arange(BLOCK)[:, None]
    col = np.arange(2 * BLOCK)[None, :]
    return jnp.asarray(np.where((col >= BLOCK) | (key > col), -1.0, 0.0), BF16)


def _attention(q4, k4, v4, sa4):
    b, pairs, s, _ = q4.shape
    assert s % (UNIT * WINDOW_GROUP) == 0 and WINDOW_GROUP % UNITS_PER_BLOCK == 0
    assert WINDOW_GROUP * UNIT >= WINDOW_KEYS
    seq_spec = pl.BlockSpec((1, 1, s, LANES), lambda bi, pi: (bi, pi, 0, 0))
    uu = _cumsum_weights()
    return pl.pallas_call(
        _attn_kernel,
        grid=(b, pairs),
        in_specs=[seq_spec, seq_spec, seq_spec, seq_spec,
                  pl.BlockSpec(uu.shape, lambda bi, pi: (0, 0))],
        out_specs=seq_spec,
        out_shape=jax.ShapeDtypeStruct(q4.shape, BF16),
        scratch_shapes=[
            pltpu.VMEM((2 * BLOCK, LANES), BF16),
            pltpu.VMEM((2 * BLOCK, LANES), F32),
            pltpu.VMEM((2 * BLOCK, BLOCK), F32),
            pltpu.VMEM((s // BLOCK, UNITS_PER_BLOCK * SUBLANES, BLOCK), F32),
        ],
        compiler_params=pltpu.CompilerParams(
            dimension_semantics=("parallel", "parallel"), vmem_limit_bytes=VMEM_LIMIT_BYTES),
        name="stickbreak_attn",
    )(q4, k4, v4, sa4, uu)


def _shift_rows(cur, prev_tail, shift):
    rolled = pltpu.roll(cur, shift, 0)
    tail = pltpu.roll(prev_tail, shift, 0)
    row = lax.broadcasted_iota(jnp.int32, tail.shape, 0)
    top = jnp.where(row < shift, tail, rolled[:SUBLANES])
    return jnp.concatenate([top, rolled[SUBLANES:]], axis=0)


def _out_kernel(oa_ref, cu_ref, gz_ref, sga_ref, sgv_ref, x_ref, cuh_ref,
                cw_ref, wpa_ref, wpc_ref, wo_ref, gain_ref, bias_ref, out_ref):
    ti = pl.program_id(1)
    rows = out_ref.shape[1] // OUT_SUBTILES
    cw = cw_ref[...]

    def branch_inputs(n):
        lo = n * rows
        o_attn = jnp.concatenate(
            [oa_ref[0, p, lo:lo + rows, :] for p in range(HEAD_PAIRS)], axis=1)
        cu = cu_ref[0, lo:lo + rows, :].astype(F32)
        if n == 0:
            prev = jnp.where(ti > 0, cuh_ref[0].astype(F32), 0.0)
        else:
            prev = cu_ref[0, lo - SUBLANES:lo, :].astype(F32)
        conv = cw[CONV_KERNEL - 1:CONV_KERNEL] * cu
        for tap in range(CONV_KERNEL - 1):
            conv = conv + cw[tap:tap + 1] * _shift_rows(cu, prev, CONV_KERNEL - 1 - tap)
        o_conv = gz_ref[0, lo:lo + rows, :].astype(F32) * conv
        return o_attn, o_conv.astype(BF16)

    def merge(n, y_attn, y_conv):
        lo = n * rows
        merged = (sga_ref[0, lo:lo + rows, :].astype(F32) * y_attn
                  + sgv_ref[0, lo:lo + rows, :].astype(F32) * y_conv)
        return merged.astype(BF16)

    def finish(n, sub):
        lo = n * rows
        h = DEEPNORM_ALPHA * x_ref[0, lo:lo + rows, :] + sub
        mu = jnp.mean(h, axis=-1, keepdims=True)
        cen = h - mu
        var = jnp.mean(cen * cen, axis=-1, keepdims=True)
        out_ref[0, lo:lo + rows, :] = cen * lax.rsqrt(var + LN_EPS) * gain_ref[...] + bias_ref[...]

    subtiles = range(OUT_SUBTILES)
    branches = [branch_inputs(n) for n in subtiles]
    wpa = wpa_ref[...].astype(BF16)
    wpc = wpc_ref[...].astype(BF16)
    wo = wo_ref[...].astype(BF16)
    projected = [(jnp.dot(a, wpa, preferred_element_type=F32),
                  jnp.dot(c, wpc, preferred_element_type=F32)) for a, c in branches]
    merged = [merge(n, ya, yc) for n, (ya, yc) in zip(subtiles, projected)]
    subs = [jnp.dot(m, wo, preferred_element_type=F32) for m in merged]
    for n, sub in zip(subtiles, subs):
        finish(n, sub)


def _out_stage(oa4, cu, gz, sga, sgv, x, conv_w, wpa, wpc, wo, gain, bias):
    b, s, d = x.shape
    tm = OUT_ROW_TILE
    grid = (b, s // tm)
    halo_blocks = tm // SUBLANES
    half_spec = pl.BlockSpec((1, tm, SB_WIDTH), lambda bi, ti: (bi, ti, 0))
    full_spec = pl.BlockSpec((1, tm, D_MODEL), lambda bi, ti: (bi, ti, 0))
    halo_spec = pl.BlockSpec(
        (1, SUBLANES, CONV_WIDTH), lambda bi, ti: (bi, jnp.maximum(ti * halo_blocks - 1, 0), 0))

    def whole(a):
        return pl.BlockSpec(a.shape, lambda bi, ti: (0,) * a.ndim)

    return pl.pallas_call(
        _out_kernel,
        grid=grid,
        in_specs=[
            pl.BlockSpec((1, HEAD_PAIRS, tm, LANES), lambda bi, ti: (bi, 0, ti, 0)),
            half_spec, half_spec,
            full_spec, full_spec, full_spec,
            halo_spec,
            whole(conv_w), whole(wpa), whole(wpc), whole(wo), whole(gain), whole(bias),
        ],
        out_specs=full_spec,
        out_shape=jax.ShapeDtypeStruct((b, s, d), x.dtype),
        compiler_params=pltpu.CompilerParams(
            dimension_semantics=("parallel", "parallel"), vmem_limit_bytes=VMEM_LIMIT_BYTES),
        name="out_stage",
    )(oa4, cu, gz, sga, sgv, x, cu, conv_w, wpa, wpc, wo, gain, bias)


@jax.jit
def kernel(x, w_in, conv_w, w_proj_attn, w_proj_conv, w_out, ln_gain, ln_bias):
    h = x
    for layer in range(DEPTH):
        q4, k4, v4, sa4, cu, gz, sga, sgv = _in_proj(h, w_in[layer])
        oa4 = _attention(q4, k4, v4, sa4)
        h = _out_stage(
            oa4, cu, gz, sga, sgv, h,
            conv_w[layer], w_proj_attn[layer], w_proj_conv[layer], w_out[layer],
            ln_gain[layer][None, :], ln_bias[layer][None, :])
    return h
```

```python
import jax
import jax.numpy as jnp
import numpy as np
from jax import lax
from jax.experimental import pallas as pl
from jax.experimental.pallas import tpu as pltpu

D_MODEL = 1024
SB_HEADS = 8
SB_HEAD_DIM = 64
SB_WIDTH = SB_HEADS * SB_HEAD_DIM
CONV_WIDTH = D_MODEL // 2
CONV_KERNEL = 3
DEPTH = 1
DEEPNORM_ALPHA = (2.0 * DEPTH) ** 0.25
LN_EPS = 1e-5

LANES = 128
SUBLANES = 8
HEAD_PAIRS = SB_WIDTH // LANES
BLOCK = 128
UNIT = 64
UNITS_PER_BLOCK = BLOCK // UNIT
WINDOW_BLOCKS = 2
WINDOW_KEYS = WINDOW_BLOCKS * BLOCK
WINDOW_GROUP = 64
IN_ROW_TILE = 1024
W_CHUNK = 1024
OUT_ROW_TILE = 1024
OUT_SUBTILES = 4
VMEM_LIMIT_BYTES = 56 * 1024 * 1024
LOG2_E = 1.4426950408889634
LOG2_SURV_CUTOFF = -104.0 * LOG2_E
MASKED_SCORE = -1e4
LARGE = 1e30

BF16 = jnp.bfloat16
F32 = jnp.float32


def _sigmoid(x):
    return 0.5 + 0.5 * jnp.tanh(0.5 * x)


def _silu(x):
    return x * _sigmoid(x)


def _in_proj_kernel(x_ref, w_hbm, q_ref, k_ref, v_ref, sa_ref, cu_ref, gz_ref, sga_ref, sgv_ref,
                    w_ref, stage_ref, sem):
    n_chunks = w_hbm.shape[1] // W_CHUNK

    def chunk_copy(c):
        slot = c % 2
        return pltpu.make_async_copy(
            w_hbm.at[:, pl.ds(c * W_CHUNK, W_CHUNK)], stage_ref.at[slot], sem.at[slot])

    @pl.when(jnp.logical_and(pl.program_id(0) == 0, pl.program_id(1) == 0))
    def _():
        chunk_copy(0).start()
        for c in range(n_chunks):
            if c + 1 < n_chunks:
                chunk_copy(c + 1).start()
            chunk_copy(c).wait()
            w_ref[:, c * W_CHUNK:(c + 1) * W_CHUNK] = stage_ref[c % 2].astype(BF16)

    xb = x_ref[0].astype(BF16)

    def seg(index, width=SB_WIDTH):
        c0 = index * SB_WIDTH
        return jnp.dot(xb, w_ref[:, c0:c0 + width], preferred_element_type=F32)

    def store_pairs(ref, y):
        for p in range(HEAD_PAIRS):
            ref[0, p] = y[:, p * LANES:(p + 1) * LANES].astype(BF16)

    sga_ref[0] = _sigmoid(seg(8, D_MODEL)).astype(BF16)
    store_pairs(q_ref, seg(0) * (SB_HEAD_DIM ** -0.5 * LOG2_E))
    sgv_ref[0] = _sigmoid(seg(10, D_MODEL)).astype(BF16)
    store_pairs(k_ref, seg(1))
    store_pairs(sa_ref, _silu(seg(3)))
    store_pairs(v_ref, seg(2))
    gate_b = seg(5)
    gz_ref[0] = (gate_b * _silu(seg(7))).astype(BF16)
    u = seg(4)
    cu_ref[0] = (seg(6) * u).astype(BF16)


def _in_proj(x, w_in):
    b, s, d = x.shape
    tm = IN_ROW_TILE
    grid = (b, s // tm)
    pair_shape = jax.ShapeDtypeStruct((b, HEAD_PAIRS, s, LANES), BF16)
    half_shape = jax.ShapeDtypeStruct((b, s, SB_WIDTH), BF16)
    full_shape = jax.ShapeDtypeStruct((b, s, D_MODEL), BF16)
    pair_spec = pl.BlockSpec((1, HEAD_PAIRS, tm, LANES), lambda bi, ti: (bi, 0, ti, 0))
    half_spec = pl.BlockSpec((1, tm, SB_WIDTH), lambda bi, ti: (bi, ti, 0))
    full_spec = pl.BlockSpec((1, tm, D_MODEL), lambda bi, ti: (bi, ti, 0))
    return pl.pallas_call(
        _in_proj_kernel,
        grid=grid,
        in_specs=[
            pl.BlockSpec((1, tm, d), lambda bi, ti: (bi, ti, 0)),
            pl.BlockSpec(memory_space=pl.ANY),
        ],
        out_specs=[pair_spec] * 4 + [half_spec] * 2 + [full_spec] * 2,
        out_shape=[pair_shape] * 4 + [half_shape] * 2 + [full_shape] * 2,
        scratch_shapes=[
            pltpu.VMEM(w_in.shape, BF16),
            pltpu.VMEM((2, w_in.shape[0], W_CHUNK), F32),
            pltpu.SemaphoreType.DMA((2,)),
        ],
        compiler_params=pltpu.CompilerParams(
            dimension_semantics=("arbitrary", "arbitrary"), vmem_limit_bytes=VMEM_LIMIT_BYTES),
        name="in_proj",
    )(x, w_in)


def _softplus2(z):
    return jnp.maximum(z, 0.0) + jnp.log2(1.0 + jnp.exp2(jnp.minimum(z, -z)))


def _attn_kernel(q_ref, k_ref, v_ref, sa_ref, uu_ref, o_ref, qs_ref, acc_ref, ls_ref, flag_ref):
    s_len = q_ref.shape[2]
    nb = s_len // BLOCK

    def first_head(rows):
        return lax.broadcasted_iota(jnp.int32, (rows, LANES), 1) < SB_HEAD_DIM

    def offset(index, size):
        if isinstance(index, int):
            return index * size
        return pl.multiple_of(index * size, size)

    def stack_heads(q2):
        zero = jnp.zeros_like(q2)
        keep = first_head(q2.shape[0])
        return jnp.concatenate([jnp.where(keep, q2, zero), jnp.where(keep, zero, q2)], axis=0)

    def store_gated(rows, acc):
        n = acc.shape[0] // 2
        o = jnp.where(first_head(n), acc[:n], acc[n:])
        o_ref[0, 0, rows, :] = (o * sa_ref[0, 0, rows, :].astype(F32)).astype(o_ref.dtype)

    newest_first = list(reversed(range(WINDOW_BLOCKS)))
    back_units = (WINDOW_KEYS - UNIT) // UNIT
    unit_row = lax.broadcasted_iota(jnp.int32, (2 * UNIT, BLOCK), 0) & (UNIT - 1)
    key_minus_row = lax.broadcasted_iota(jnp.int32, (2 * UNIT, BLOCK), 1) - unit_row

    def window_start(u):
        if isinstance(u, int):
            return max(u - back_units, 0)
        return u - back_units

    def key_bounds(u):
        first = window_start(u) if isinstance(u, int) else None
        bounds = []
        for b in range(WINDOW_BLOCKS):
            lead = (back_units if first is None else u - first) * UNIT - b * BLOCK
            bounds.append(None if lead >= BLOCK else lead)
        return bounds

    def scores(u):
        q2 = q_ref[0, 0, pl.ds(offset(u, UNIT), UNIT), :]
        kw = k_ref[0, 0, pl.ds(offset(window_start(u), UNIT), WINDOW_KEYS), :]
        return lax.dot_general(stack_heads(q2), kw, (((1,), (1,)), ((), ())),
                               preferred_element_type=F32)

    def split_logs(z, bounds):
        sp, log_hit = [], []
        for b, bound in enumerate(bounds):
            zb = z[:, b * BLOCK:(b + 1) * BLOCK]
            if bound is not None:
                zb = jnp.where(key_minus_row < bound, zb, MASKED_SCORE)
            softplus = _softplus2(zb)
            sp.append(softplus.astype(BF16))
            log_hit.append(zb - softplus)
        return jnp.concatenate([sp[b] for b in newest_first], axis=0), log_hit

    def later_sums(sp):
        return jnp.dot(sp, uu_ref[...], preferred_element_type=F32)

    def weights_of(sums, log_hit):
        log_surv = None
        weights = [None] * WINDOW_BLOCKS
        for n, b in enumerate(newest_first):
            block_sums = sums[n * 2 * UNIT:(n + 1) * 2 * UNIT]
            log_a = log_hit[b] + block_sums[:, :BLOCK]
            if log_surv is not None:
                log_a = log_a + log_surv
            weights[b] = jnp.exp2(log_a).astype(BF16)
            total = block_sums[:, BLOCK:]
            log_surv = total if log_surv is None else log_surv + total
        return jnp.concatenate(weights, axis=1), log_surv

    def finish(u, slot, weights, log_surv):
        vw = v_ref[0, 0, pl.ds(offset(window_start(u), UNIT), WINDOW_KEYS), :]
        acc = jnp.dot(weights, vw, preferred_element_type=F32)
        store_gated(pl.ds(offset(u, UNIT), UNIT), acc)
        if isinstance(u, int) and window_start(u) == 0:
            flag = jnp.full((SUBLANES, BLOCK), -LARGE, F32)
        else:
            flag = jnp.max(log_surv.reshape(2 * UNIT // SUBLANES, SUBLANES, BLOCK), axis=0)
        block, half = slot
        flag_ref[block, half * SUBLANES:(half + 1) * SUBLANES, :] = flag

    def window_group(g):
        units = [g * WINDOW_GROUP + n for n in range(WINDOW_GROUP)]
        slots = [(g * (WINDOW_GROUP // UNITS_PER_BLOCK) + n // UNITS_PER_BLOCK, n % UNITS_PER_BLOCK)
                 for n in range(WINDOW_GROUP)]
        zs = [scores(u) for u in units]
        logs = [split_logs(z, key_bounds(u)) for u, z in zip(units, zs)]
        sums = [later_sums(sp) for sp, _ in logs]
        probs = [weights_of(s, log_hit) for s, (_, log_hit) in zip(sums, logs)]
        for u, slot, (w, log_surv) in zip(units, slots, probs):
            finish(u, slot, w, log_surv)

    def traced_group(g, carry):
        window_group(g)
        return carry

    window_group(0)
    lax.fori_loop(1, s_len // (UNIT * WINDOW_GROUP), traced_group, 0)

    block_row = lax.broadcasted_iota(jnp.int32, (2 * BLOCK, BLOCK), 0) & (BLOCK - 1)
    strict = lax.broadcasted_iota(jnp.int32, (2 * BLOCK, BLOCK), 1) < block_row

    def tile(j, masked):
        k2 = k_ref[0, 0, pl.ds(offset(j, BLOCK), BLOCK), :]
        v2 = v_ref[0, 0, pl.ds(offset(j, BLOCK), BLOCK), :]
        z = lax.dot_general(qs_ref[...], k2, (((1,), (1,)), ((), ())),
                            preferred_element_type=F32)
        if masked:
            z = jnp.where(strict, z, MASKED_SCORE)
        softplus = _softplus2(z)
        sums = jnp.dot(softplus.astype(BF16), uu_ref[...], preferred_element_type=F32)
        log_surv = ls_ref[...]
        a = jnp.exp2(z - softplus + sums[:, :BLOCK] + log_surv)
        acc_ref[...] += jnp.dot(a.astype(BF16), v2, preferred_element_type=F32)
        ls_ref[...] = log_surv + sums[:, BLOCK:]

    def general_block(i):
        rows = pl.ds(offset(i, BLOCK), BLOCK)
        qs_ref[...] = stack_heads(q_ref[0, 0, rows, :])
        acc_ref[...] = jnp.zeros_like(acc_ref)
        ls_ref[...] = jnp.zeros_like(ls_ref)
        tile(i, True)

        def more(c):
            j, worst = c
            return jnp.logical_and(j >= 0, worst > LOG2_SURV_CUTOFF)

        def step(c):
            j, _ = c
            tile(j, False)
            return j - 1, jnp.max(ls_ref[...])

        lax.while_loop(more, step, (i - 1, jnp.max(ls_ref[...])))
        store_gated(rows, acc_ref[...])

    @pl.when(jnp.max(flag_ref[...]) > LOG2_SURV_CUTOFF)
    def _():
        def maybe_redo(i, carry):
            @pl.when(jnp.max(flag_ref[i]) > LOG2_SURV_CUTOFF)
            def _():
                general_block(i)
            return carry

        lax.fori_loop(0, nb, maybe_redo, 0)


def _cumsum_weights():
    key = np.arange(BLOCK)[:, None]
    col = np.arange(2 * BLOCK)[None, :]
    return jnp.asarray(np.where((col >= BLOCK) | (key > col), -1.0, 0.0), BF16)


def _attention(q4, k4, v4, sa4):
    b, pairs, s, _ = q4.shape
    assert s % (UNIT * WINDOW_GROUP) == 0 and WINDOW_GROUP % UNITS_PER_BLOCK == 0
    assert WINDOW_GROUP * UNIT >= WINDOW_KEYS
    seq_spec = pl.BlockSpec((1, 1, s, LANES), lambda bi, pi: (bi, pi, 0, 0))
    uu = _cumsum_weights()
    return pl.pallas_call(
        _attn_kernel,
        grid=(b, pairs),
        in_specs=[seq_spec, seq_spec, seq_spec, seq_spec,
                  pl.BlockSpec(uu.shape, lambda bi, pi: (0, 0))],
        out_specs=seq_spec,
        out_shape=jax.ShapeDtypeStruct(q4.shape, BF16),
        scratch_shapes=[
            pltpu.VMEM((2 * BLOCK, LANES), BF16),
            pltpu.VMEM((2 * BLOCK, LANES), F32),
            pltpu.VMEM((2 * BLOCK, BLOCK), F32),
            pltpu.VMEM((s // BLOCK, UNITS_PER_BLOCK * SUBLANES, BLOCK), F32),
        ],
        compiler_params=pltpu.CompilerParams(
            dimension_semantics=("parallel", "parallel"), vmem_limit_bytes=VMEM_LIMIT_BYTES),
        name="stickbreak_attn",
    )(q4, k4, v4, sa4, uu)


def _shift_rows(cur, prev_tail, shift):
    rolled = pltpu.roll(cur, shift, 0)
    tail = pltpu.roll(prev_tail, shift, 0)
    row = lax.broadcasted_iota(jnp.int32, tail.shape, 0)
    top = jnp.where(row < shift, tail, rolled[:SUBLANES])
    return jnp.concatenate([top, rolled[SUBLANES:]], axis=0)


def _out_kernel(oa_ref, cu_ref, gz_ref, sga_ref, sgv_ref, x_ref, cuh_ref,
                cw_ref, wpa_ref, wpc_ref, wo_ref, gain_ref, bias_ref, out_ref):
    ti = pl.program_id(1)
    rows = out_ref.shape[1] // OUT_SUBTILES
    cw = cw_ref[...]

    def branch_inputs(n):
        lo = n * rows
        o_attn = jnp.concatenate(
            [oa_ref[0, p, lo:lo + rows, :] for p in range(HEAD_PAIRS)], axis=1)
        cu = cu_ref[0, lo:lo + rows, :].astype(F32)
        if n == 0:
            prev = jnp.where(ti > 0, cuh_ref[0].astype(F32), 0.0)
        else:
            prev = cu_ref[0, lo - SUBLANES:lo, :].astype(F32)
        conv = cw[CONV_KERNEL - 1:CONV_KERNEL] * cu
        for tap in range(CONV_KERNEL - 1):
            conv = conv + cw[tap:tap + 1] * _shift_rows(cu, prev, CONV_KERNEL - 1 - tap)
        o_conv = gz_ref[0, lo:lo + rows, :].astype(F32) * conv
        return o_attn, o_conv.astype(BF16)

    def merge(n, y_attn, y_conv):
        lo = n * rows
        merged = (sga_ref[0, lo:lo + rows, :].astype(F32) * y_attn
                  + sgv_ref[0, lo:lo + rows, :].astype(F32) * y_conv)
        return merged.astype(BF16)

    def finish(n, sub):
        lo = n * rows
        h = DEEPNORM_ALPHA * x_ref[0, lo:lo + rows, :] + sub
        mu = jnp.mean(h, axis=-1, keepdims=True)
        cen = h - mu
        var = jnp.mean(cen * cen, axis=-1, keepdims=True)
        out_ref[0, lo:lo + rows, :] = cen * lax.rsqrt(var + LN_EPS) * gain_ref[...] + bias_ref[...]

    subtiles = range(OUT_SUBTILES)
    branches = [branch_inputs(n) for n in subtiles]
    wpa = wpa_ref[...].astype(BF16)
    wpc = wpc_ref[...].astype(BF16)
    wo = wo_ref[...].astype(BF16)
    projected = [(jnp.dot(a, wpa, preferred_element_type=F32),
                  jnp.dot(c, wpc, preferred_element_type=F32)) for a, c in branches]
    merged = [merge(n, ya, yc) for n, (ya, yc) in zip(subtiles, projected)]
    subs = [jnp.dot(m, wo, preferred_element_type=F32) for m in merged]
    for n, sub in zip(subtiles, subs):
        finish(n, sub)


def _out_stage(oa4, cu, gz, sga, sgv, x, conv_w, wpa, wpc, wo, gain, bias):
    b, s, d = x.shape
    tm = OUT_ROW_TILE
    grid = (b, s // tm)
    halo_blocks = tm // SUBLANES
    half_spec = pl.BlockSpec((1, tm, SB_WIDTH), lambda bi, ti: (bi, ti, 0))
    full_spec = pl.BlockSpec((1, tm, D_MODEL), lambda bi, ti: (bi, ti, 0))
    halo_spec = pl.BlockSpec(
        (1, SUBLANES, CONV_WIDTH), lambda bi, ti: (bi, jnp.maximum(ti * halo_blocks - 1, 0), 0))

    def whole(a):
        return pl.BlockSpec(a.shape, lambda bi, ti: (0,) * a.ndim)

    return pl.pallas_call(
        _out_kernel,
        grid=grid,
        in_specs=[
            pl.BlockSpec((1, HEAD_PAIRS, tm, LANES), lambda bi, ti: (bi, 0, ti, 0)),
            half_spec, half_spec,
            full_spec, full_spec, full_spec,
            halo_spec,
            whole(conv_w), whole(wpa), whole(wpc), whole(wo), whole(gain), whole(bias),
        ],
        out_specs=full_spec,
        out_shape=jax.ShapeDtypeStruct((b, s, d), x.dtype),
        compiler_params=pltpu.CompilerParams(
            dimension_semantics=("parallel", "parallel"), vmem_limit_bytes=VMEM_LIMIT_BYTES),
        name="out_stage",
    )(oa4, cu, gz, sga, sgv, x, cu, conv_w, wpa, wpc, wo, gain, bias)


@jax.jit
def kernel(x, w_in, conv_w, w_proj_attn, w_proj_conv, w_out, ln_gain, ln_bias):
    h = x
    for layer in range(DEPTH):
        q4, k4, v4, sa4, cu, gz, sga, sgv = _in_proj(h, w_in[layer])
        oa4 = _attention(q4, k4, v4, sa4)
        h = _out_stage(
            oa4, cu, gz, sga, sgv, h,
            conv_w[layer], w_proj_attn[layer], w_proj_conv[layer], w_out[layer],
            ln_gain[layer][None, :], ln_bias[layer][None, :])
    return h
```

```python
import jax
import jax.numpy as jnp
import numpy as np
from jax import lax
from jax.experimental import pallas as pl
from jax.experimental.pallas import tpu as pltpu

D_MODEL = 1024
SB_HEADS = 8
SB_HEAD_DIM = 64
SB_WIDTH = SB_HEADS * SB_HEAD_DIM
CONV_WIDTH = D_MODEL // 2
CONV_KERNEL = 3
DEPTH = 1
DEEPNORM_ALPHA = (2.0 * DEPTH) ** 0.25
LN_EPS = 1e-5

LANES = 128
SUBLANES = 8
HEAD_PAIRS = SB_WIDTH // LANES
BLOCK = 128
UNIT = 64
UNITS_PER_BLOCK = BLOCK // UNIT
WINDOW_BLOCKS = 2
WINDOW_KEYS = WINDOW_BLOCKS * BLOCK
WINDOW_GROUP = 64
IN_ROW_TILE = 1024
W_CHUNK = 1024
OUT_ROW_TILE = 1024
OUT_SUBTILES = 4
VMEM_LIMIT_BYTES = 56 * 1024 * 1024
LOG2_E = 1.4426950408889634
LOG2_SURV_CUTOFF = -104.0 * LOG2_E
MASKED_SCORE = -1e4
LARGE = 1e30

BF16 = jnp.bfloat16
F32 = jnp.float32


def _sigmoid(x):
    return 0.5 + 0.5 * jnp.tanh(0.5 * x)


def _silu(x):
    return x * _sigmoid(x)


def _in_proj_kernel(x_ref, w_hbm, q_ref, k_ref, v_ref, sa_ref, cu_ref, gz_ref, sga_ref, sgv_ref,
                    w_ref, stage_ref, sem):
    n_chunks = w_hbm.shape[1] // W_CHUNK

    def chunk_copy(c):
        slot = c % 2
        return pltpu.make_async_copy(
            w_hbm.at[:, pl.ds(c * W_CHUNK, W_CHUNK)], stage_ref.at[slot], sem.at[slot])

    @pl.when(jnp.logical_and(pl.program_id(0) == 0, pl.program_id(1) == 0))
    def _():
        chunk_copy(0).start()
        for c in range(n_chunks):
            if c + 1 < n_chunks:
                chunk_copy(c + 1).start()
            chunk_copy(c).wait()
            w_ref[:, c * W_CHUNK:(c + 1) * W_CHUNK] = stage_ref[c % 2].astype(BF16)

    xb = x_ref[0].astype(BF16)

    def seg(index, width=SB_WIDTH):
        c0 = index * SB_WIDTH
        return jnp.dot(xb, w_ref[:, c0:c0 + width], preferred_element_type=F32)

    def store_pairs(ref, y):
        for p in range(HEAD_PAIRS):
            ref[0, p] = y[:, p * LANES:(p + 1) * LANES].astype(BF16)

    sga_ref[0] = _sigmoid(seg(8, D_MODEL)).astype(BF16)
    store_pairs(q_ref, seg(0) * (SB_HEAD_DIM ** -0.5 * LOG2_E))
    sgv_ref[0] = _sigmoid(seg(10, D_MODEL)).astype(BF16)
    store_pairs(k_ref, seg(1))
    store_pairs(sa_ref, _silu(seg(3)))
    store_pairs(v_ref, seg(2))
    gate_b = seg(5)
    gz_ref[0] = (gate_b * _silu(seg(7))).astype(BF16)
    u = seg(4)
    cu_ref[0] = (seg(6) * u).astype(BF16)


def _in_proj(x, w_in):
    b, s, d = x.shape
    tm = IN_ROW_TILE
    grid = (b, s // tm)
    pair_shape = jax.ShapeDtypeStruct((b, HEAD_PAIRS, s, LANES), BF16)
    half_shape = jax.ShapeDtypeStruct((b, s, SB_WIDTH), BF16)
    full_shape = jax.ShapeDtypeStruct((b, s, D_MODEL), BF16)
    pair_spec = pl.BlockSpec((1, HEAD_PAIRS, tm, LANES), lambda bi, ti: (bi, 0, ti, 0))
    half_spec = pl.BlockSpec((1, tm, SB_WIDTH), lambda bi, ti: (bi, ti, 0))
    full_spec = pl.BlockSpec((1, tm, D_MODEL), lambda bi, ti: (bi, ti, 0))
    return pl.pallas_call(
        _in_proj_kernel,
        grid=grid,
        in_specs=[
            pl.BlockSpec((1, tm, d), lambda bi, ti: (bi, ti, 0)),
            pl.BlockSpec(memory_space=pl.ANY),
        ],
        out_specs=[pair_spec] * 4 + [half_spec] * 2 + [full_spec] * 2,
        out_shape=[pair_shape] * 4 + [half_shape] * 2 + [full_shape] * 2,
        scratch_shapes=[
            pltpu.VMEM(w_in.shape, BF16),
            pltpu.VMEM((2, w_in.shape[0], W_CHUNK), F32),
            pltpu.SemaphoreType.DMA((2,)),
        ],
        compiler_params=pltpu.CompilerParams(
            dimension_semantics=("arbitrary", "arbitrary"), vmem_limit_bytes=VMEM_LIMIT_BYTES),
        name="in_proj",
    )(x, w_in)


def _softplus2(z):
    return jnp.maximum(z, 0.0) + jnp.log2(1.0 + jnp.exp2(jnp.minimum(z, -z)))


def _attn_kernel(q_ref, k_ref, v_ref, sa_ref, uu_ref, o_ref, qs_ref, acc_ref, ls_ref, flag_ref):
    s_len = q_ref.shape[2]
    nb = s_len // BLOCK

    def first_head(rows):
        return lax.broadcasted_iota(jnp.int32, (rows, LANES), 1) < SB_HEAD_DIM

    def offset(index, size):
        if isinstance(index, int):
            return index * size
        return pl.multiple_of(index * size, size)

    def stack_heads(q2):
        zero = jnp.zeros_like(q2)
        keep = first_head(q2.shape[0])
        return jnp.concatenate([jnp.where(keep, q2, zero), jnp.where(keep, zero, q2)], axis=0)

    def store_gated(rows, acc):
        n = acc.shape[0] // 2
        o = jnp.where(first_head(n), acc[:n], acc[n:])
        o_ref[0, 0, rows, :] = (o * sa_ref[0, 0, rows, :].astype(F32)).astype(o_ref.dtype)

    back_units = (WINDOW_KEYS - UNIT) // UNIT
    unit_row = lax.broadcasted_iota(jnp.int32, (2 * UNIT, BLOCK), 0) & (UNIT - 1)
    key_minus_row = lax.broadcasted_iota(jnp.int32, (2 * UNIT, BLOCK), 1) - unit_row

    def window_start(u):
        if isinstance(u, int):
            return max(u - back_units, 0)
        return u - back_units

    def key_bounds(u):
        first = window_start(u) if isinstance(u, int) else None
        bounds = []
        for b in range(WINDOW_BLOCKS):
            lead = (back_units if first is None else u - first) * UNIT - b * BLOCK
            bounds.append(None if lead >= BLOCK else lead)
        return bounds

    def scores(u):
        q2 = q_ref[0, 0, pl.ds(offset(u, UNIT), UNIT), :]
        kw = k_ref[0, 0, pl.ds(offset(window_start(u), UNIT), WINDOW_KEYS), :]
        return lax.dot_general(stack_heads(q2), kw, (((1,), (1,)), ((), ())),
                               preferred_element_type=F32)

    def split_logs(z, bounds):
        sp, log_hit = [], []
        for b, bound in enumerate(bounds):
            zb = z[:, b * BLOCK:(b + 1) * BLOCK]
            if bound is not None:
                zb = jnp.where(key_minus_row < bound, zb, MASKED_SCORE)
            softplus = _softplus2(zb)
            sp.append(softplus.astype(BF16))
            log_hit.append(zb - softplus)
        return sp, log_hit

    def later_sums(sp):
        older = jnp.dot(jnp.concatenate(sp, axis=1), uu_ref[...], preferred_element_type=F32)
        newer = jnp.dot(sp[1], uu_ref[:BLOCK, :BLOCK], preferred_element_type=F32)
        return older, newer

    def weights_of(sums, log_hit):
        older, newer = sums
        weights = [jnp.exp2(log_hit[0] + older[:, :BLOCK]).astype(BF16),
                   jnp.exp2(log_hit[1] + newer).astype(BF16)]
        return jnp.concatenate(weights, axis=1), older[:, BLOCK:]

    def finish(u, slot, weights, log_surv):
        vw = v_ref[0, 0, pl.ds(offset(window_start(u), UNIT), WINDOW_KEYS), :]
        acc = jnp.dot(weights, vw, preferred_element_type=F32)
        store_gated(pl.ds(offset(u, UNIT), UNIT), acc)
        if isinstance(u, int) and window_start(u) == 0:
            flag = jnp.full((SUBLANES, BLOCK), -LARGE, F32)
        else:
            flag = jnp.max(log_surv.reshape(2 * UNIT // SUBLANES, SUBLANES, BLOCK), axis=0)
        block, half = slot
        flag_ref[block, half * SUBLANES:(half + 1) * SUBLANES, :] = flag

    def window_group(g):
        units = [g * WINDOW_GROUP + n for n in range(WINDOW_GROUP)]
        slots = [(g * (WINDOW_GROUP // UNITS_PER_BLOCK) + n // UNITS_PER_BLOCK, n % UNITS_PER_BLOCK)
                 for n in range(WINDOW_GROUP)]
        zs = [scores(u) for u in units]
        logs = [split_logs(z, key_bounds(u)) for u, z in zip(units, zs)]
        sums = [later_sums(sp) for sp, _ in logs]
        probs = [weights_of(s, log_hit) for s, (_, log_hit) in zip(sums, logs)]
        for u, slot, (w, log_surv) in zip(units, slots, probs):
            finish(u, slot, w, log_surv)

    def traced_group(g, carry):
        window_group(g)
        return carry

    window_group(0)
    lax.fori_loop(1, s_len // (UNIT * WINDOW_GROUP), traced_group, 0)

    block_row = lax.broadcasted_iota(jnp.int32, (2 * BLOCK, BLOCK), 0) & (BLOCK - 1)
    strict = lax.broadcasted_iota(jnp.int32, (2 * BLOCK, BLOCK), 1) < block_row

    def tile(j, masked):
        k2 = k_ref[0, 0, pl.ds(offset(j, BLOCK), BLOCK), :]
        v2 = v_ref[0, 0, pl.ds(offset(j, BLOCK), BLOCK), :]
        z = lax.dot_general(qs_ref[...], k2, (((1,), (1,)), ((), ())),
                            preferred_element_type=F32)
        if masked:
            z = jnp.where(strict, z, MASKED_SCORE)
        softplus = _softplus2(z)
        sums = jnp.dot(softplus.astype(BF16), uu_ref[:BLOCK, :], preferred_element_type=F32)
        log_surv = ls_ref[...]
        a = jnp.exp2(z - softplus + sums[:, :BLOCK] + log_surv)
        acc_ref[...] += jnp.dot(a.astype(BF16), v2, preferred_element_type=F32)
        ls_ref[...] = log_surv + sums[:, BLOCK:]

    def general_block(i):
        rows = pl.ds(offset(i, BLOCK), BLOCK)
        qs_ref[...] = stack_heads(q_ref[0, 0, rows, :])
        acc_ref[...] = jnp.zeros_like(acc_ref)
        ls_ref[...] = jnp.zeros_like(ls_ref)
        tile(i, True)

        def more(c):
            j, worst = c
            return jnp.logical_and(j >= 0, worst > LOG2_SURV_CUTOFF)

        def step(c):
            j, _ = c
            tile(j, False)
            return j - 1, jnp.max(ls_ref[...])

        lax.while_loop(more, step, (i - 1, jnp.max(ls_ref[...])))
        store_gated(rows, acc_ref[...])

    @pl.when(jnp.max(flag_ref[...]) > LOG2_SURV_CUTOFF)
    def _():
        def maybe_redo(i, carry):
            @pl.when(jnp.max(flag_ref[i]) > LOG2_SURV_CUTOFF)
            def _():
                general_block(i)
            return carry

        lax.fori_loop(0, nb, maybe_redo, 0)


def _cumsum_weights():
    key = np.arange(2 * BLOCK)[:, None]
    col = np.arange(2 * BLOCK)[None, :]
    return jnp.asarray(np.where((col >= BLOCK) | (key > col), -1.0, 0.0), BF16)


def _attention(q4, k4, v4, sa4):
    b, pairs, s, _ = q4.shape
    assert s % (UNIT * WINDOW_GROUP) == 0 and WINDOW_GROUP % UNITS_PER_BLOCK == 0
    assert WINDOW_GROUP * UNIT >= WINDOW_KEYS
    assert WINDOW_BLOCKS == 2
    seq_spec = pl.BlockSpec((1, 1, s, LANES), lambda bi, pi: (bi, pi, 0, 0))
    uu = _cumsum_weights()
    return pl.pallas_call(
        _attn_kernel,
        grid=(b, pairs),
        in_specs=[seq_spec, seq_spec, seq_spec, seq_spec,
                  pl.BlockSpec(uu.shape, lambda bi, pi: (0, 0))],
        out_specs=seq_spec,
        out_shape=jax.ShapeDtypeStruct(q4.shape, BF16),
        scratch_shapes=[
            pltpu.VMEM((2 * BLOCK, LANES), BF16),
            pltpu.VMEM((2 * BLOCK, LANES), F32),
            pltpu.VMEM((2 * BLOCK, BLOCK), F32),
            pltpu.VMEM((s // BLOCK, UNITS_PER_BLOCK * SUBLANES, BLOCK), F32),
        ],
        compiler_params=pltpu.CompilerParams(
            dimension_semantics=("parallel", "parallel"), vmem_limit_bytes=VMEM_LIMIT_BYTES),
        name="stickbreak_attn",
    )(q4, k4, v4, sa4, uu)


def _shift_rows(cur, prev_tail, shift):
    rolled = pltpu.roll(cur, shift, 0)
    tail = pltpu.roll(prev_tail, shift, 0)
    row = lax.broadcasted_iota(jnp.int32, tail.shape, 0)
    top = jnp.where(row < shift, tail, rolled[:SUBLANES])
    return jnp.concatenate([top, rolled[SUBLANES:]], axis=0)


def _out_kernel(oa_ref, cu_ref, gz_ref, sga_ref, sgv_ref, x_ref, cuh_ref,
                cw_ref, wpa_ref, wpc_ref, wo_ref, gain_ref, bias_ref, out_ref):
    ti = pl.program_id(1)
    rows = out_ref.shape[1] // OUT_SUBTILES
    cw = cw_ref[...]

    def branch_inputs(n):
        lo = n * rows
        o_attn = jnp.concatenate(
            [oa_ref[0, p, lo:lo + rows, :] for p in range(HEAD_PAIRS)], axis=1)
        cu = cu_ref[0, lo:lo + rows, :].astype(F32)
        if n == 0:
            prev = jnp.where(ti > 0, cuh_ref[0].astype(F32), 0.0)
        else:
            prev = cu_ref[0, lo - SUBLANES:lo, :].astype(F32)
        conv = cw[CONV_KERNEL - 1:CONV_KERNEL] * cu
        for tap in range(CONV_KERNEL - 1):
            conv = conv + cw[tap:tap + 1] * _shift_rows(cu, prev, CONV_KERNEL - 1 - tap)
        o_conv = gz_ref[0, lo:lo + rows, :].astype(F32) * conv
        return o_attn, o_conv.astype(BF16)

    def merge(n, y_attn, y_conv):
        lo = n * rows
        merged = (sga_ref[0, lo:lo + rows, :].astype(F32) * y_attn
                  + sgv_ref[0, lo:lo + rows, :].astype(F32) * y_conv)
        return merged.astype(BF16)

    def finish(n, sub):
        lo = n * rows
        h = DEEPNORM_ALPHA * x_ref[0, lo:lo + rows, :] + sub
        mu = jnp.mean(h, axis=-1, keepdims=True)
        cen = h - mu
        var = jnp.mean(cen * cen, axis=-1, keepdims=True)
        out_ref[0, lo:lo + rows, :] = cen * lax.rsqrt(var + LN_EPS) * gain_ref[...] + bias_ref[...]

    subtiles = range(OUT_SUBTILES)
    branches = [branch_inputs(n) for n in subtiles]
    wpa = wpa_ref[...].astype(BF16)
    wpc = wpc_ref[...].astype(BF16)
    wo = wo_ref[...].astype(BF16)
    projected = [(jnp.dot(a, wpa, preferred_element_type=F32),
                  jnp.dot(c, wpc, preferred_element_type=F32)) for a, c in branches]
    merged = [merge(n, ya, yc) for n, (ya, yc) in zip(subtiles, projected)]
    subs = [jnp.dot(m, wo, preferred_element_type=F32) for m in merged]
    for n, sub in zip(subtiles, subs):
        finish(n, sub)


def _out_stage(oa4, cu, gz, sga, sgv, x, conv_w, wpa, wpc, wo, gain, bias):
    b, s, d = x.shape
    tm = OUT_ROW_TILE
    grid = (b, s // tm)
    halo_blocks = tm // SUBLANES
    half_spec = pl.BlockSpec((1, tm, SB_WIDTH), lambda bi, ti: (bi, ti, 0))
    full_spec = pl.BlockSpec((1, tm, D_MODEL), lambda bi, ti: (bi, ti, 0))
    halo_spec = pl.BlockSpec(
        (1, SUBLANES, CONV_WIDTH), lambda bi, ti: (bi, jnp.maximum(ti * halo_blocks - 1, 0), 0))

    def whole(a):
        return pl.BlockSpec(a.shape, lambda bi, ti: (0,) * a.ndim)

    return pl.pallas_call(
        _out_kernel,
        grid=grid,
        in_specs=[
            pl.BlockSpec((1, HEAD_PAIRS, tm, LANES), lambda bi, ti: (bi, 0, ti, 0)),
            half_spec, half_spec,
            full_spec, full_spec, full_spec,
            halo_spec,
            whole(conv_w), whole(wpa), whole(wpc), whole(wo), whole(gain), whole(bias),
        ],
        out_specs=full_spec,
        out_shape=jax.ShapeDtypeStruct((b, s, d), x.dtype),
        compiler_params=pltpu.CompilerParams(
            dimension_semantics=("parallel", "parallel"), vmem_limit_bytes=VMEM_LIMIT_BYTES),
        name="out_stage",
    )(oa4, cu, gz, sga, sgv, x, cu, conv_w, wpa, wpc, wo, gain, bias)


@jax.jit
def kernel(x, w_in, conv_w, w_proj_attn, w_proj_conv, w_out, ln_gain, ln_bias):
    h = x
    for layer in range(DEPTH):
        q4, k4, v4, sa4, cu, gz, sga, sgv = _in_proj(h, w_in[layer])
        oa4 = _attention(q4, k4, v4, sa4)
        h = _out_stage(
            oa4, cu, gz, sga, sgv, h,
            conv_w[layer], w_proj_attn[layer], w_proj_conv[layer], w_out[layer],
            ln_gain[layer][None, :], ln_bias[layer][None, :])
    return h
```

```python
import jax
import jax.numpy as jnp
import numpy as np
from jax import lax
from jax.experimental import pallas as pl
from jax.experimental.pallas import tpu as pltpu

D_MODEL = 1024
SB_HEADS = 8
SB_HEAD_DIM = 64
SB_WIDTH = SB_HEADS * SB_HEAD_DIM
CONV_WIDTH = D_MODEL // 2
CONV_KERNEL = 3
DEPTH = 1
DEEPNORM_ALPHA = (2.0 * DEPTH) ** 0.25
LN_EPS = 1e-5

LANES = 128
SUBLANES = 8
HEAD_PAIRS = SB_WIDTH // LANES
BLOCK = 128
UNIT = 64
UNITS_PER_BLOCK = BLOCK // UNIT
WINDOW_BLOCKS = 2
WINDOW_KEYS = WINDOW_BLOCKS * BLOCK
WINDOW_GROUP = 64
IN_ROW_TILE = 1024
W_CHUNK = 1024
OUT_ROW_TILE = 1024
OUT_SUBTILES = 4
VMEM_LIMIT_BYTES = 56 * 1024 * 1024
LOG2_E = 1.4426950408889634
LOG2_SURV_CUTOFF = -104.0 * LOG2_E
MASKED_SCORE = -1e4
LARGE = 1e30

BF16 = jnp.bfloat16
F32 = jnp.float32


def _sigmoid(x):
    return 0.5 + 0.5 * jnp.tanh(0.5 * x)


def _silu(x):
    return x * _sigmoid(x)


def _in_proj_kernel(x_ref, w_hbm, q_ref, k_ref, v_ref, sa_ref, cu_ref, gz_ref, sga_ref, sgv_ref,
                    w_ref, stage_ref, sem):
    n_chunks = w_hbm.shape[1] // W_CHUNK

    def chunk_copy(c):
        slot = c % 2
        return pltpu.make_async_copy(
            w_hbm.at[:, pl.ds(c * W_CHUNK, W_CHUNK)], stage_ref.at[slot], sem.at[slot])

    @pl.when(jnp.logical_and(pl.program_id(0) == 0, pl.program_id(1) == 0))
    def _():
        chunk_copy(0).start()
        for c in range(n_chunks):
            if c + 1 < n_chunks:
                chunk_copy(c + 1).start()
            chunk_copy(c).wait()
            w_ref[:, c * W_CHUNK:(c + 1) * W_CHUNK] = stage_ref[c % 2].astype(BF16)

    xb = x_ref[0].astype(BF16)

    def seg(index, width=SB_WIDTH):
        c0 = index * SB_WIDTH
        return jnp.dot(xb, w_ref[:, c0:c0 + width], preferred_element_type=F32)

    def store_pairs(ref, y):
        for p in range(HEAD_PAIRS):
            ref[0, p] = y[:, p * LANES:(p + 1) * LANES].astype(BF16)

    sga_ref[0] = _sigmoid(seg(8, D_MODEL)).astype(BF16)
    store_pairs(q_ref, seg(0) * (SB_HEAD_DIM ** -0.5 * LOG2_E))
    sgv_ref[0] = _sigmoid(seg(10, D_MODEL)).astype(BF16)
    store_pairs(k_ref, seg(1))
    store_pairs(sa_ref, _silu(seg(3)))
    store_pairs(v_ref, seg(2))
    gate_b = seg(5)
    gz_ref[0] = (gate_b * _silu(seg(7))).astype(BF16)
    u = seg(4)
    cu_ref[0] = (seg(6) * u).astype(BF16)


def _in_proj(x, w_in):
    b, s, d = x.shape
    tm = IN_ROW_TILE
    grid = (b, s // tm)
    pair_shape = jax.ShapeDtypeStruct((b, HEAD_PAIRS, s, LANES), BF16)
    half_shape = jax.ShapeDtypeStruct((b, s, SB_WIDTH), BF16)
    full_shape = jax.ShapeDtypeStruct((b, s, D_MODEL), BF16)
    pair_spec = pl.BlockSpec((1, HEAD_PAIRS, tm, LANES), lambda bi, ti: (bi, 0, ti, 0))
    half_spec = pl.BlockSpec((1, tm, SB_WIDTH), lambda bi, ti: (bi, ti, 0))
    full_spec = pl.BlockSpec((1, tm, D_MODEL), lambda bi, ti: (bi, ti, 0))
    return pl.pallas_call(
        _in_proj_kernel,
        grid=grid,
        in_specs=[
            pl.BlockSpec((1, tm, d), lambda bi, ti: (bi, ti, 0)),
            pl.BlockSpec(memory_space=pl.ANY),
        ],
        out_specs=[pair_spec] * 4 + [half_spec] * 2 + [full_spec] * 2,
        out_shape=[pair_shape] * 4 + [half_shape] * 2 + [full_shape] * 2,
        scratch_shapes=[
            pltpu.VMEM(w_in.shape, BF16),
            pltpu.VMEM((2, w_in.shape[0], W_CHUNK), F32),
            pltpu.SemaphoreType.DMA((2,)),
        ],
        compiler_params=pltpu.CompilerParams(
            dimension_semantics=("arbitrary", "arbitrary"), vmem_limit_bytes=VMEM_LIMIT_BYTES),
        name="in_proj",
    )(x, w_in)


def _softplus2(z):
    return jnp.maximum(z, 0.0) + jnp.log2(1.0 + jnp.exp2(jnp.minimum(z, -z)))


def _attn_kernel(q_ref, k_ref, v_ref, sa_ref, uu_ref, o_ref, qs_ref, acc_ref, ls_ref, flag_ref):
    s_len = q_ref.shape[2]
    nb = s_len // BLOCK

    def first_head(rows):
        return lax.broadcasted_iota(jnp.int32, (rows, LANES), 1) < SB_HEAD_DIM

    def offset(index, size):
        if isinstance(index, int):
            return index * size
        return pl.multiple_of(index * size, size)

    def stack_heads(q2):
        zero = jnp.zeros_like(q2)
        keep = first_head(q2.shape[0])
        return jnp.concatenate([jnp.where(keep, q2, zero), jnp.where(keep, zero, q2)], axis=0)

    def store_gated(rows, acc):
        n = acc.shape[0] // 2
        o = jnp.where(first_head(n), acc[:n], acc[n:])
        o_ref[0, 0, rows, :] = (o * sa_ref[0, 0, rows, :].astype(F32)).astype(o_ref.dtype)

    back_units = (WINDOW_KEYS - UNIT) // UNIT
    unit_row = lax.broadcasted_iota(jnp.int32, (2 * UNIT, BLOCK), 0) & (UNIT - 1)
    key_minus_row = lax.broadcasted_iota(jnp.int32, (2 * UNIT, BLOCK), 1) - unit_row

    def window_start(u):
        if isinstance(u, int):
            return max(u - back_units, 0)
        return u - back_units

    def key_bounds(u):
        first = window_start(u) if isinstance(u, int) else None
        bounds = []
        for b in range(WINDOW_BLOCKS):
            lead = (back_units if first is None else u - first) * UNIT - b * BLOCK
            bounds.append(None if lead >= BLOCK else lead)
        return bounds

    def scores(u):
        q2 = q_ref[0, 0, pl.ds(offset(u, UNIT), UNIT), :]
        kw = k_ref[0, 0, pl.ds(offset(window_start(u), UNIT), WINDOW_KEYS), :]
        return lax.dot_general(stack_heads(q2), kw, (((1,), (1,)), ((), ())),
                               preferred_element_type=F32)

    def split_logs(z, bounds):
        sp, log_hit = [], []
        for b, bound in enumerate(bounds):
            zb = z[:, b * BLOCK:(b + 1) * BLOCK]
            if bound is not None:
                zb = jnp.where(key_minus_row < bound, zb, MASKED_SCORE)
            softplus = _softplus2(zb)
            sp.append(softplus.astype(BF16))
            log_hit.append(zb - softplus)
        return sp, log_hit

    def later_sums(sp):
        older = jnp.dot(jnp.concatenate(sp, axis=1), uu_ref[...], preferred_element_type=F32)
        newer = jnp.dot(sp[1], uu_ref[:BLOCK, :BLOCK], preferred_element_type=F32)
        return older, newer

    def weights_of(sums, log_hit):
        older, newer = sums
        weights = [jnp.exp2(log_hit[0] + older[:, :BLOCK]).astype(BF16),
                   jnp.exp2(log_hit[1] + newer).astype(BF16)]
        return jnp.concatenate(weights, axis=1), older[:, BLOCK:]

    def finish(u, slot, weights, log_surv):
        vw = v_ref[0, 0, pl.ds(offset(window_start(u), UNIT), WINDOW_KEYS), :]
        acc = jnp.dot(weights, vw, preferred_element_type=F32)
        store_gated(pl.ds(offset(u, UNIT), UNIT), acc)
        if isinstance(u, int) and window_start(u) == 0:
            flag = jnp.full((SUBLANES, BLOCK), -LARGE, F32)
        else:
            flag = jnp.max(log_surv.reshape(2 * UNIT // SUBLANES, SUBLANES, BLOCK), axis=0)
        block, half = slot
        flag_ref[block, half * SUBLANES:(half + 1) * SUBLANES, :] = flag

    def window_group(g):
        units = [g * WINDOW_GROUP + n for n in range(WINDOW_GROUP)]
        slots = [(g * (WINDOW_GROUP // UNITS_PER_BLOCK) + n // UNITS_PER_BLOCK, n % UNITS_PER_BLOCK)
                 for n in range(WINDOW_GROUP)]
        zs = [scores(u) for u in units]
        logs = [split_logs(z, key_bounds(u)) for u, z in zip(units, zs)]
        sums = [later_sums(sp) for sp, _ in logs]
        probs = [weights_of(s, log_hit) for s, (_, log_hit) in zip(sums, logs)]
        for u, slot, (w, log_surv) in zip(units, slots, probs):
            finish(u, slot, w, log_surv)

    def traced_group(g, carry):
        window_group(g)
        return carry

    window_group(0)
    lax.fori_loop(1, s_len // (UNIT * WINDOW_GROUP), traced_group, 0)

    block_row = lax.broadcasted_iota(jnp.int32, (2 * BLOCK, BLOCK), 0) & (BLOCK - 1)
    strict = lax.broadcasted_iota(jnp.int32, (2 * BLOCK, BLOCK), 1) < block_row

    def tile(j, masked):
        k2 = k_ref[0, 0, pl.ds(offset(j, BLOCK), BLOCK), :]
        v2 = v_ref[0, 0, pl.ds(offset(j, BLOCK), BLOCK), :]
        z = lax.dot_general(qs_ref[...], k2, (((1,), (1,)), ((), ())),
                            preferred_element_type=F32)
        if masked:
            z = jnp.where(strict, z, MASKED_SCORE)
        softplus = _softplus2(z)
        sums = jnp.dot(softplus.astype(BF16), uu_ref[:BLOCK, :], preferred_element_type=F32)
        log_surv = ls_ref[...]
        a = jnp.exp2(z - softplus + sums[:, :BLOCK] + log_surv)
        acc_ref[...] += jnp.dot(a.astype(BF16), v2, preferred_element_type=F32)
        ls_ref[...] = log_surv + sums[:, BLOCK:]

    def general_block(i):
        rows = pl.ds(offset(i, BLOCK), BLOCK)
        qs_ref[...] = stack_heads(q_ref[0, 0, rows, :])
        acc_ref[...] = jnp.zeros_like(acc_ref)
        ls_ref[...] = jnp.zeros_like(ls_ref)
        tile(i, True)

        def more(c):
            j, worst = c
            return jnp.logical_and(j >= 0, worst > LOG2_SURV_CUTOFF)

        def step(c):
            j, _ = c
            tile(j, False)
            return j - 1, jnp.max(ls_ref[...])

        lax.while_loop(more, step, (i - 1, jnp.max(ls_ref[...])))
        store_gated(rows, acc_ref[...])

    @pl.when(jnp.max(flag_ref[...]) > LOG2_SURV_CUTOFF)
    def _():
        def maybe_redo(i, carry):
            @pl.when(jnp.max(flag_ref[i]) > LOG2_SURV_CUTOFF)
            def _():
                general_block(i)
            return carry

        lax.fori_loop(0, nb, maybe_redo, 0)


def _cumsum_weights():
    key = np.arange(2 * BLOCK)[:, None]
    col = np.arange(2 * BLOCK)[None, :]
    return jnp.asarray(np.where((col >= BLOCK) | (key > col), -1.0, 0.0), BF16)


def _attention(q4, k4, v4, sa4):
    b, pairs, s, _ = q4.shape
    assert s % (UNIT * WINDOW_GROUP) == 0 and WINDOW_GROUP % UNITS_PER_BLOCK == 0
    assert WINDOW_GROUP * UNIT >= WINDOW_KEYS
    assert WINDOW_BLOCKS == 2
    seq_spec = pl.BlockSpec((1, 1, s, LANES), lambda bi, pi: (bi, pi, 0, 0))
    uu = _cumsum_weights()
    return pl.pallas_call(
        _attn_kernel,
        grid=(b, pairs),
        in_specs=[seq_spec, seq_spec, seq_spec, seq_spec,
                  pl.BlockSpec(uu.shape, lambda bi, pi: (0, 0))],
        out_specs=seq_spec,
        out_shape=jax.ShapeDtypeStruct(q4.shape, BF16),
        scratch_shapes=[
            pltpu.VMEM((2 * BLOCK, LANES), BF16),
            pltpu.VMEM((2 * BLOCK, LANES), F32),
            pltpu.VMEM((2 * BLOCK, BLOCK), F32),
            pltpu.VMEM((s // BLOCK, UNITS_PER_BLOCK * SUBLANES, BLOCK), F32),
        ],
        compiler_params=pltpu.CompilerParams(
            dimension_semantics=("parallel", "parallel"), vmem_limit_bytes=VMEM_LIMIT_BYTES),
        name="stickbreak_attn",
    )(q4, k4, v4, sa4, uu)


def _shift_rows(cur, prev_tail, shift):
    rolled = pltpu.roll(cur, shift, 0)
    tail = pltpu.roll(prev_tail, shift, 0)
    row = lax.broadcasted_iota(jnp.int32, tail.shape, 0)
    top = jnp.where(row < shift, tail, rolled[:SUBLANES])
    return jnp.concatenate([top, rolled[SUBLANES:]], axis=0)


def _out_kernel(oa_ref, cu_ref, gz_ref, sga_ref, sgv_ref, x_ref, cuh_ref,
                cw_ref, wpa_ref, wpc_ref, wo_ref, gain_ref, bias_ref, out_hbm, out_buf, out_sem):
    ti = pl.program_id(1)
    tiles = pl.num_programs(1)
    tm = x_ref.shape[1]
    step = pl.program_id(0) * tiles + ti
    last = pl.num_programs(0) * tiles - 1
    rows = tm // OUT_SUBTILES
    cw = cw_ref[...]

    def writeback(of_step, n):
        slot = of_step % 2
        dst = out_hbm.at[of_step // tiles,
                         pl.ds(pl.multiple_of((of_step % tiles) * tm + n * rows, rows), rows), :]
        return pltpu.make_async_copy(out_buf.at[slot, n * rows:(n + 1) * rows, :], dst,
                                     out_sem.at[slot, n])

    def wait_writebacks(of_step):
        for n in range(OUT_SUBTILES):
            writeback(of_step, n).wait()

    @pl.when(step >= 2)
    def _():
        wait_writebacks(step - 2)

    def branch_inputs(n):
        lo = n * rows
        o_attn = jnp.concatenate(
            [oa_ref[0, p, lo:lo + rows, :] for p in range(HEAD_PAIRS)], axis=1)
        cu = cu_ref[0, lo:lo + rows, :].astype(F32)
        if n == 0:
            prev = jnp.where(ti > 0, cuh_ref[0].astype(F32), 0.0)
        else:
            prev = cu_ref[0, lo - SUBLANES:lo, :].astype(F32)
        conv = cw[CONV_KERNEL - 1:CONV_KERNEL] * cu
        for tap in range(CONV_KERNEL - 1):
            conv = conv + cw[tap:tap + 1] * _shift_rows(cu, prev, CONV_KERNEL - 1 - tap)
        o_conv = gz_ref[0, lo:lo + rows, :].astype(F32) * conv
        return o_attn, o_conv.astype(BF16)

    def merge(n, y_attn, y_conv):
        lo = n * rows
        merged = (sga_ref[0, lo:lo + rows, :].astype(F32) * y_attn
                  + sgv_ref[0, lo:lo + rows, :].astype(F32) * y_conv)
        return merged.astype(BF16)

    def finish(n, sub):
        lo = n * rows
        h = DEEPNORM_ALPHA * x_ref[0, lo:lo + rows, :] + sub
        mu = jnp.mean(h, axis=-1, keepdims=True)
        cen = h - mu
        var = jnp.mean(cen * cen, axis=-1, keepdims=True)
        out_buf[step % 2, lo:lo + rows, :] = cen * lax.rsqrt(var + LN_EPS) * gain_ref[...] + bias_ref[...]
        writeback(step, n).start()

    subtiles = range(OUT_SUBTILES)
    branches = [branch_inputs(n) for n in subtiles]
    wpa = wpa_ref[...].astype(BF16)
    wpc = wpc_ref[...].astype(BF16)
    wo = wo_ref[...].astype(BF16)
    projected = [(jnp.dot(a, wpa, preferred_element_type=F32),
                  jnp.dot(c, wpc, preferred_element_type=F32)) for a, c in branches]
    merged = [merge(n, ya, yc) for n, (ya, yc) in zip(subtiles, projected)]
    subs = [jnp.dot(m, wo, preferred_element_type=F32) for m in merged]
    for n, sub in zip(subtiles, subs):
        finish(n, sub)

    @pl.when(jnp.logical_and(step == last, step >= 1))
    def _():
        wait_writebacks(step - 1)

    @pl.when(step == last)
    def _():
        wait_writebacks(step)


def _out_stage(oa4, cu, gz, sga, sgv, x, conv_w, wpa, wpc, wo, gain, bias):
    b, s, d = x.shape
    tm = OUT_ROW_TILE
    grid = (b, s // tm)
    halo_blocks = tm // SUBLANES
    half_spec = pl.BlockSpec((1, tm, SB_WIDTH), lambda bi, ti: (bi, ti, 0))
    full_spec = pl.BlockSpec((1, tm, D_MODEL), lambda bi, ti: (bi, ti, 0))
    halo_spec = pl.BlockSpec(
        (1, SUBLANES, CONV_WIDTH), lambda bi, ti: (bi, jnp.maximum(ti * halo_blocks - 1, 0), 0))

    def whole(a):
        return pl.BlockSpec(a.shape, lambda bi, ti: (0,) * a.ndim)

    return pl.pallas_call(
        _out_kernel,
        grid=grid,
        in_specs=[
            pl.BlockSpec((1, HEAD_PAIRS, tm, LANES), lambda bi, ti: (bi, 0, ti, 0)),
            half_spec, half_spec,
            full_spec, full_spec, full_spec,
            halo_spec,
            whole(conv_w), whole(wpa), whole(wpc), whole(wo), whole(gain), whole(bias),
        ],
        out_specs=pl.BlockSpec(memory_space=pl.ANY),
        out_shape=jax.ShapeDtypeStruct((b, s, d), x.dtype),
        scratch_shapes=[
            pltpu.VMEM((2, tm, d), x.dtype),
            pltpu.SemaphoreType.DMA((2, OUT_SUBTILES)),
        ],
        compiler_params=pltpu.CompilerParams(
            dimension_semantics=("arbitrary", "arbitrary"), vmem_limit_bytes=VMEM_LIMIT_BYTES),
        name="out_stage",
    )(oa4, cu, gz, sga, sgv, x, cu, conv_w, wpa, wpc, wo, gain, bias)


@jax.jit
def kernel(x, w_in, conv_w, w_proj_attn, w_proj_conv, w_out, ln_gain, ln_bias):
    h = x
    for layer in range(DEPTH):
        q4, k4, v4, sa4, cu, gz, sga, sgv = _in_proj(h, w_in[layer])
        oa4 = _attention(q4, k4, v4, sa4)
        h = _out_stage(
            oa4, cu, gz, sga, sgv, h,
            conv_w[layer], w_proj_attn[layer], w_proj_conv[layer], w_out[layer],
            ln_gain[layer][None, :], ln_bias[layer][None, :])
    return h
```

```python
import jax
import jax.numpy as jnp
import numpy as np
from jax import lax
from jax.experimental import pallas as pl
from jax.experimental.pallas import tpu as pltpu

D_MODEL = 1024
SB_HEADS = 8
SB_HEAD_DIM = 64
SB_WIDTH = SB_HEADS * SB_HEAD_DIM
CONV_WIDTH = D_MODEL // 2
CONV_KERNEL = 3
DEPTH = 1
DEEPNORM_ALPHA = (2.0 * DEPTH) ** 0.25
LN_EPS = 1e-5

LANES = 128
SUBLANES = 8
HEAD_PAIRS = SB_WIDTH // LANES
BLOCK = 128
UNIT = 64
UNITS_PER_BLOCK = BLOCK // UNIT
WINDOW_BLOCKS = 2
WINDOW_KEYS = WINDOW_BLOCKS * BLOCK
WINDOW_GROUP = 64
IN_ROW_TILE = 1024
W_CHUNK = 1024
OUT_ROW_TILE = 1024
OUT_SUBTILES = 4
X_RING_SLOTS = 3
VMEM_LIMIT_BYTES = 56 * 1024 * 1024
LOG2_E = 1.4426950408889634
LOG2_SURV_CUTOFF = -104.0 * LOG2_E
MASKED_SCORE = -1e4
LARGE = 1e30

BF16 = jnp.bfloat16
F32 = jnp.float32


def _sigmoid(x):
    return 0.5 + 0.5 * jnp.tanh(0.5 * x)


def _silu(x):
    return x * _sigmoid(x)


def _in_proj_kernel(x_ref, w_hbm, q_ref, k_ref, v_ref, sa_ref, cu_ref, gz_ref, sga_ref, sgv_ref,
                    w_ref, stage_ref, sem):
    n_chunks = w_hbm.shape[1] // W_CHUNK

    def chunk_copy(c):
        slot = c % 2
        return pltpu.make_async_copy(
            w_hbm.at[:, pl.ds(c * W_CHUNK, W_CHUNK)], stage_ref.at[slot], sem.at[slot])

    @pl.when(jnp.logical_and(pl.program_id(0) == 0, pl.program_id(1) == 0))
    def _():
        chunk_copy(0).start()
        for c in range(n_chunks):
            if c + 1 < n_chunks:
                chunk_copy(c + 1).start()
            chunk_copy(c).wait()
            w_ref[:, c * W_CHUNK:(c + 1) * W_CHUNK] = stage_ref[c % 2].astype(BF16)

    xb = x_ref[0].astype(BF16)

    def seg(index, width=SB_WIDTH):
        c0 = index * SB_WIDTH
        return jnp.dot(xb, w_ref[:, c0:c0 + width], preferred_element_type=F32)

    def store_pairs(ref, y):
        for p in range(HEAD_PAIRS):
            ref[0, p] = y[:, p * LANES:(p + 1) * LANES].astype(BF16)

    sga_ref[0] = _sigmoid(seg(8, D_MODEL)).astype(BF16)
    store_pairs(q_ref, seg(0) * (SB_HEAD_DIM ** -0.5 * LOG2_E))
    sgv_ref[0] = _sigmoid(seg(10, D_MODEL)).astype(BF16)
    store_pairs(k_ref, seg(1))
    store_pairs(sa_ref, _silu(seg(3)))
    store_pairs(v_ref, seg(2))
    gate_b = seg(5)
    gz_ref[0] = (gate_b * _silu(seg(7))).astype(BF16)
    u = seg(4)
    cu_ref[0] = (seg(6) * u).astype(BF16)


def _in_proj(x, w_in):
    b, s, d = x.shape
    tm = IN_ROW_TILE
    grid = (b, s // tm)
    pair_shape = jax.ShapeDtypeStruct((b, HEAD_PAIRS, s, LANES), BF16)
    half_shape = jax.ShapeDtypeStruct((b, s, SB_WIDTH), BF16)
    full_shape = jax.ShapeDtypeStruct((b, s, D_MODEL), BF16)
    pair_spec = pl.BlockSpec((1, HEAD_PAIRS, tm, LANES), lambda bi, ti: (bi, 0, ti, 0))
    half_spec = pl.BlockSpec((1, tm, SB_WIDTH), lambda bi, ti: (bi, ti, 0))
    full_spec = pl.BlockSpec((1, tm, D_MODEL), lambda bi, ti: (bi, ti, 0))
    return pl.pallas_call(
        _in_proj_kernel,
        grid=grid,
        in_specs=[
            pl.BlockSpec((1, tm, d), lambda bi, ti: (bi, ti, 0)),
            pl.BlockSpec(memory_space=pl.ANY),
        ],
        out_specs=[pair_spec] * 4 + [half_spec] * 2 + [full_spec] * 2,
        out_shape=[pair_shape] * 4 + [half_shape] * 2 + [full_shape] * 2,
        scratch_shapes=[
            pltpu.VMEM(w_in.shape, BF16),
            pltpu.VMEM((2, w_in.shape[0], W_CHUNK), F32),
            pltpu.SemaphoreType.DMA((2,)),
        ],
        compiler_params=pltpu.CompilerParams(
            dimension_semantics=("arbitrary", "arbitrary"), vmem_limit_bytes=VMEM_LIMIT_BYTES),
        name="in_proj",
    )(x, w_in)


def _softplus2(z):
    return jnp.maximum(z, 0.0) + jnp.log2(1.0 + jnp.exp2(jnp.minimum(z, -z)))


def _attn_kernel(q_ref, k_ref, v_ref, sa_ref, uu_ref, o_ref, qs_ref, acc_ref, ls_ref, flag_ref):
    s_len = q_ref.shape[2]
    nb = s_len // BLOCK

    def first_head(rows):
        return lax.broadcasted_iota(jnp.int32, (rows, LANES), 1) < SB_HEAD_DIM

    def offset(index, size):
        if isinstance(index, int):
            return index * size
        return pl.multiple_of(index * size, size)

    def stack_heads(q2):
        zero = jnp.zeros_like(q2)
        keep = first_head(q2.shape[0])
        return jnp.concatenate([jnp.where(keep, q2, zero), jnp.where(keep, zero, q2)], axis=0)

    def store_gated(rows, acc):
        n = acc.shape[0] // 2
        o = jnp.where(first_head(n), acc[:n], acc[n:])
        o_ref[0, 0, rows, :] = (o * sa_ref[0, 0, rows, :].astype(F32)).astype(o_ref.dtype)

    back_units = (WINDOW_KEYS - UNIT) // UNIT
    unit_row = lax.broadcasted_iota(jnp.int32, (2 * UNIT, BLOCK), 0) & (UNIT - 1)
    key_minus_row = lax.broadcasted_iota(jnp.int32, (2 * UNIT, BLOCK), 1) - unit_row

    def window_start(u):
        if isinstance(u, int):
            return max(u - back_units, 0)
        return u - back_units

    def key_bounds(u):
        first = window_start(u) if isinstance(u, int) else None
        bounds = []
        for b in range(WINDOW_BLOCKS):
            lead = (back_units if first is None else u - first) * UNIT - b * BLOCK
            bounds.append(None if lead >= BLOCK else lead)
        return bounds

    def scores(u):
        q2 = q_ref[0, 0, pl.ds(offset(u, UNIT), UNIT), :]
        kw = k_ref[0, 0, pl.ds(offset(window_start(u), UNIT), WINDOW_KEYS), :]
        return lax.dot_general(stack_heads(q2), kw, (((1,), (1,)), ((), ())),
                               preferred_element_type=F32)

    def split_logs(z, bounds):
        sp, log_hit = [], []
        for b, bound in enumerate(bounds):
            zb = z[:, b * BLOCK:(b + 1) * BLOCK]
            if bound is not None:
                zb = jnp.where(key_minus_row < bound, zb, MASKED_SCORE)
            softplus = _softplus2(zb)
            sp.append(softplus.astype(BF16))
            log_hit.append(zb - softplus)
        return sp, log_hit

    def later_sums(sp):
        older = jnp.dot(jnp.concatenate(sp, axis=1), uu_ref[...], preferred_element_type=F32)
        newer = jnp.dot(sp[1], uu_ref[:BLOCK, :BLOCK], preferred_element_type=F32)
        return older, newer

    def weights_of(sums, log_hit):
        older, newer = sums
        weights = [jnp.exp2(log_hit[0] + older[:, :BLOCK]).astype(BF16),
                   jnp.exp2(log_hit[1] + newer).astype(BF16)]
        return jnp.concatenate(weights, axis=1), older[:, BLOCK:]

    def finish(u, slot, weights, log_surv):
        vw = v_ref[0, 0, pl.ds(offset(window_start(u), UNIT), WINDOW_KEYS), :]
        acc = jnp.dot(weights, vw, preferred_element_type=F32)
        store_gated(pl.ds(offset(u, UNIT), UNIT), acc)
        if isinstance(u, int) and window_start(u) == 0:
            flag = jnp.full((SUBLANES, BLOCK), -LARGE, F32)
        else:
            flag = jnp.max(log_surv.reshape(2 * UNIT // SUBLANES, SUBLANES, BLOCK), axis=0)
        block, half = slot
        flag_ref[block, half * SUBLANES:(half + 1) * SUBLANES, :] = flag

    def window_group(g):
        units = [g * WINDOW_GROUP + n for n in range(WINDOW_GROUP)]
        slots = [(g * (WINDOW_GROUP // UNITS_PER_BLOCK) + n // UNITS_PER_BLOCK, n % UNITS_PER_BLOCK)
                 for n in range(WINDOW_GROUP)]
        zs = [scores(u) for u in units]
        logs = [split_logs(z, key_bounds(u)) for u, z in zip(units, zs)]
        sums = [later_sums(sp) for sp, _ in logs]
        probs = [weights_of(s, log_hit) for s, (_, log_hit) in zip(sums, logs)]
        for u, slot, (w, log_surv) in zip(units, slots, probs):
            finish(u, slot, w, log_surv)

    def traced_group(g, carry):
        window_group(g)
        return carry

    window_group(0)
    lax.fori_loop(1, s_len // (UNIT * WINDOW_GROUP), traced_group, 0)

    block_row = lax.broadcasted_iota(jnp.int32, (2 * BLOCK, BLOCK), 0) & (BLOCK - 1)
    strict = lax.broadcasted_iota(jnp.int32, (2 * BLOCK, BLOCK), 1) < block_row

    def tile(j, masked):
        k2 = k_ref[0, 0, pl.ds(offset(j, BLOCK), BLOCK), :]
        v2 = v_ref[0, 0, pl.ds(offset(j, BLOCK), BLOCK), :]
        z = lax.dot_general(qs_ref[...], k2, (((1,), (1,)), ((), ())),
                            preferred_element_type=F32)
        if masked:
            z = jnp.where(strict, z, MASKED_SCORE)
        softplus = _softplus2(z)
        sums = jnp.dot(softplus.astype(BF16), uu_ref[:BLOCK, :], preferred_element_type=F32)
        log_surv = ls_ref[...]
        a = jnp.exp2(z - softplus + sums[:, :BLOCK] + log_surv)
        acc_ref[...] += jnp.dot(a.astype(BF16), v2, preferred_element_type=F32)
        ls_ref[...] = log_surv + sums[:, BLOCK:]

    def general_block(i):
        rows = pl.ds(offset(i, BLOCK), BLOCK)
        qs_ref[...] = stack_heads(q_ref[0, 0, rows, :])
        acc_ref[...] = jnp.zeros_like(acc_ref)
        ls_ref[...] = jnp.zeros_like(ls_ref)
        tile(i, True)

        def more(c):
            j, worst = c
            return jnp.logical_and(j >= 0, worst > LOG2_SURV_CUTOFF)

        def step(c):
            j, _ = c
            tile(j, False)
            return j - 1, jnp.max(ls_ref[...])

        lax.while_loop(more, step, (i - 1, jnp.max(ls_ref[...])))
        store_gated(rows, acc_ref[...])

    @pl.when(jnp.max(flag_ref[...]) > LOG2_SURV_CUTOFF)
    def _():
        def maybe_redo(i, carry):
            @pl.when(jnp.max(flag_ref[i]) > LOG2_SURV_CUTOFF)
            def _():
                general_block(i)
            return carry

        lax.fori_loop(0, nb, maybe_redo, 0)


def _cumsum_weights():
    key = np.arange(2 * BLOCK)[:, None]
    col = np.arange(2 * BLOCK)[None, :]
    return jnp.asarray(np.where((col >= BLOCK) | (key > col), -1.0, 0.0), BF16)


def _attention(q4, k4, v4, sa4):
    b, pairs, s, _ = q4.shape
    assert s % (UNIT * WINDOW_GROUP) == 0 and WINDOW_GROUP % UNITS_PER_BLOCK == 0
    assert WINDOW_GROUP * UNIT >= WINDOW_KEYS
    assert WINDOW_BLOCKS == 2
    seq_spec = pl.BlockSpec((1, 1, s, LANES), lambda bi, pi: (bi, pi, 0, 0))
    uu = _cumsum_weights()
    return pl.pallas_call(
        _attn_kernel,
        grid=(b, pairs),
        in_specs=[seq_spec, seq_spec, seq_spec, seq_spec,
                  pl.BlockSpec(uu.shape, lambda bi, pi: (0, 0))],
        out_specs=seq_spec,
        out_shape=jax.ShapeDtypeStruct(q4.shape, BF16),
        scratch_shapes=[
            pltpu.VMEM((2 * BLOCK, LANES), BF16),
            pltpu.VMEM((2 * BLOCK, LANES), F32),
            pltpu.VMEM((2 * BLOCK, BLOCK), F32),
            pltpu.VMEM((s // BLOCK, UNITS_PER_BLOCK * SUBLANES, BLOCK), F32),
        ],
        compiler_params=pltpu.CompilerParams(
            dimension_semantics=("parallel", "parallel"), vmem_limit_bytes=VMEM_LIMIT_BYTES),
        name="stickbreak_attn",
    )(q4, k4, v4, sa4, uu)


def _shift_rows(cur, prev_tail, shift):
    rolled = pltpu.roll(cur, shift, 0)
    tail = pltpu.roll(prev_tail, shift, 0)
    row = lax.broadcasted_iota(jnp.int32, tail.shape, 0)
    top = jnp.where(row < shift, tail, rolled[:SUBLANES])
    return jnp.concatenate([top, rolled[SUBLANES:]], axis=0)


def _out_kernel(oa_ref, cu_ref, gz_ref, sga_ref, sgv_ref, x_hbm, cuh_ref,
                cw_ref, wpa_ref, wpc_ref, wo_ref, gain_ref, bias_ref, out_hbm,
                out_buf, out_sem, x_ring, x_sem):
    ti = pl.program_id(1)
    tiles = pl.num_programs(1)
    tm = out_buf.shape[1]
    step = pl.program_id(0) * tiles + ti
    last = pl.num_programs(0) * tiles - 1
    rows = tm // OUT_SUBTILES
    cw = cw_ref[...]

    def writeback(of_step, n):
        slot = of_step % 2
        dst = out_hbm.at[of_step // tiles,
                         pl.ds(pl.multiple_of((of_step % tiles) * tm + n * rows, rows), rows), :]
        return pltpu.make_async_copy(out_buf.at[slot, n * rows:(n + 1) * rows, :], dst,
                                     out_sem.at[slot, n])

    def wait_writebacks(of_step):
        for n in range(OUT_SUBTILES):
            writeback(of_step, n).wait()

    @pl.when(step >= 2)
    def _():
        wait_writebacks(step - 2)

    def x_copy(n):
        slot = n % X_RING_SLOTS
        src = x_hbm.at[n // tiles, pl.ds(pl.multiple_of((n % tiles) * tm, tm), tm), :]
        return pltpu.make_async_copy(src, x_ring.at[slot], x_sem.at[slot])

    @pl.when(step == 0)
    def _():
        for n in range(X_RING_SLOTS - 1):
            x_copy(n).start()

    @pl.when(step + (X_RING_SLOTS - 1) <= last)
    def _():
        x_copy(step + (X_RING_SLOTS - 1)).start()

    x_copy(step).wait()
    x_tile = x_ring.at[step % X_RING_SLOTS]

    def branch_inputs(n):
        lo = n * rows
        o_attn = jnp.concatenate(
            [oa_ref[0, p, lo:lo + rows, :] for p in range(HEAD_PAIRS)], axis=1)
        cu = cu_ref[0, lo:lo + rows, :].astype(F32)
        if n == 0:
            prev = jnp.where(ti > 0, cuh_ref[0].astype(F32), 0.0)
        else:
            prev = cu_ref[0, lo - SUBLANES:lo, :].astype(F32)
        conv = cw[CONV_KERNEL - 1:CONV_KERNEL] * cu
        for tap in range(CONV_KERNEL - 1):
            conv = conv + cw[tap:tap + 1] * _shift_rows(cu, prev, CONV_KERNEL - 1 - tap)
        o_conv = gz_ref[0, lo:lo + rows, :].astype(F32) * conv
        return o_attn, o_conv.astype(BF16)

    def merge(n, y_attn, y_conv):
        lo = n * rows
        merged = (sga_ref[0, lo:lo + rows, :].astype(F32) * y_attn
                  + sgv_ref[0, lo:lo + rows, :].astype(F32) * y_conv)
        return merged.astype(BF16)

    def finish(n, sub):
        lo = n * rows
        h = DEEPNORM_ALPHA * x_tile[lo:lo + rows, :] + sub
        mu = jnp.mean(h, axis=-1, keepdims=True)
        cen = h - mu
        var = jnp.mean(cen * cen, axis=-1, keepdims=True)
        out_buf[step % 2, lo:lo + rows, :] = cen * lax.rsqrt(var + LN_EPS) * gain_ref[...] + bias_ref[...]
        writeback(step, n).start()

    subtiles = range(OUT_SUBTILES)
    branches = [branch_inputs(n) for n in subtiles]
    wpa = wpa_ref[...].astype(BF16)
    wpc = wpc_ref[...].astype(BF16)
    wo = wo_ref[...].astype(BF16)
    projected = [(jnp.dot(a, wpa, preferred_element_type=F32),
                  jnp.dot(c, wpc, preferred_element_type=F32)) for a, c in branches]
    merged = [merge(n, ya, yc) for n, (ya, yc) in zip(subtiles, projected)]
    subs = [jnp.dot(m, wo, preferred_element_type=F32) for m in merged]
    for n, sub in zip(subtiles, subs):
        finish(n, sub)

    @pl.when(jnp.logical_and(step == last, step >= 1))
    def _():
        wait_writebacks(step - 1)

    @pl.when(step == last)
    def _():
        wait_writebacks(step)


def _out_stage(oa4, cu, gz, sga, sgv, x, conv_w, wpa, wpc, wo, gain, bias):
    b, s, d = x.shape
    tm = OUT_ROW_TILE
    grid = (b, s // tm)
    halo_blocks = tm // SUBLANES
    half_spec = pl.BlockSpec((1, tm, SB_WIDTH), lambda bi, ti: (bi, ti, 0))
    full_spec = pl.BlockSpec((1, tm, D_MODEL), lambda bi, ti: (bi, ti, 0))
    halo_spec = pl.BlockSpec(
        (1, SUBLANES, CONV_WIDTH), lambda bi, ti: (bi, jnp.maximum(ti * halo_blocks - 1, 0), 0))

    def whole(a):
        return pl.BlockSpec(a.shape, lambda bi, ti: (0,) * a.ndim)

    return pl.pallas_call(
        _out_kernel,
        grid=grid,
        in_specs=[
            pl.BlockSpec((1, HEAD_PAIRS, tm, LANES), lambda bi, ti: (bi, 0, ti, 0)),
            half_spec, half_spec,
            full_spec, full_spec, pl.BlockSpec(memory_space=pl.ANY),
            halo_spec,
            whole(conv_w), whole(wpa), whole(wpc), whole(wo), whole(gain), whole(bias),
        ],
        out_specs=pl.BlockSpec(memory_space=pl.ANY),
        out_shape=jax.ShapeDtypeStruct((b, s, d), x.dtype),
        scratch_shapes=[
            pltpu.VMEM((2, tm, d), x.dtype),
            pltpu.SemaphoreType.DMA((2, OUT_SUBTILES)),
            pltpu.VMEM((X_RING_SLOTS, tm, d), x.dtype),
            pltpu.SemaphoreType.DMA((X_RING_SLOTS,)),
        ],
        compiler_params=pltpu.CompilerParams(
            dimension_semantics=("arbitrary", "arbitrary"), vmem_limit_bytes=VMEM_LIMIT_BYTES),
        name="out_stage",
    )(oa4, cu, gz, sga, sgv, x, cu, conv_w, wpa, wpc, wo, gain, bias)


@jax.jit
def kernel(x, w_in, conv_w, w_proj_attn, w_proj_conv, w_out, ln_gain, ln_bias):
    h = x
    for layer in range(DEPTH):
        q4, k4, v4, sa4, cu, gz, sga, sgv = _in_proj(h, w_in[layer])
        oa4 = _attention(q4, k4, v4, sa4)
        h = _out_stage(
            oa4, cu, gz, sga, sgv, h,
            conv_w[layer], w_proj_attn[layer], w_proj_conv[layer], w_out[layer],
            ln_gain[layer][None, :], ln_bias[layer][None, :])
    return h
```

```python
import jax
import jax.numpy as jnp
import numpy as np
from jax import lax
from jax.experimental import pallas as pl
from jax.experimental.pallas import tpu as pltpu

D_MODEL = 1024
SB_HEADS = 8
SB_HEAD_DIM = 64
SB_WIDTH = SB_HEADS * SB_HEAD_DIM
CONV_WIDTH = D_MODEL // 2
CONV_KERNEL = 3
DEPTH = 1
DEEPNORM_ALPHA = (2.0 * DEPTH) ** 0.25
LN_EPS = 1e-5

LANES = 128
SUBLANES = 8
HEAD_PAIRS = SB_WIDTH // LANES
BLOCK = 128
UNIT = 64
UNITS_PER_BLOCK = BLOCK // UNIT
WINDOW_BLOCKS = 2
WINDOW_KEYS = WINDOW_BLOCKS * BLOCK
WINDOW_GROUP = 64
IN_ROW_TILE = 1024
W_CHUNK = 1024
OUT_ROW_TILE = 1024
OUT_SUBTILES = 4
VMEM_LIMIT_BYTES = 56 * 1024 * 1024
LOG2_E = 1.4426950408889634
LOG2_SURV_CUTOFF = -104.0 * LOG2_E
MASKED_SCORE = -1e4
LARGE = 1e30

BF16 = jnp.bfloat16
F32 = jnp.float32


def _sigmoid(x):
    return 0.5 + 0.5 * jnp.tanh(0.5 * x)


def _silu(x):
    half = 0.5 * x
    return half + half * jnp.tanh(half)


def _in_proj_kernel(x_ref, w_hbm, q_ref, k_ref, v_ref, sa_ref, cu_ref, gz_ref, sga_ref, sgv_ref,
                    w_ref, stage_ref, sem):
    n_chunks = w_hbm.shape[1] // W_CHUNK

    def chunk_copy(c):
        slot = c % 2
        return pltpu.make_async_copy(
            w_hbm.at[:, pl.ds(c * W_CHUNK, W_CHUNK)], stage_ref.at[slot], sem.at[slot])

    @pl.when(jnp.logical_and(pl.program_id(0) == 0, pl.program_id(1) == 0))
    def _():
        chunk_copy(0).start()
        for c in range(n_chunks):
            if c + 1 < n_chunks:
                chunk_copy(c + 1).start()
            chunk_copy(c).wait()
            w_ref[:, c * W_CHUNK:(c + 1) * W_CHUNK] = stage_ref[c % 2].astype(BF16)

    xb = x_ref[0].astype(BF16)

    def seg(index, width=SB_WIDTH):
        c0 = index * SB_WIDTH
        return jnp.dot(xb, w_ref[:, c0:c0 + width], preferred_element_type=F32)

    def store_pairs(ref, y):
        for p in range(HEAD_PAIRS):
            ref[0, p] = y[:, p * LANES:(p + 1) * LANES].astype(BF16)

    sga_ref[0] = _sigmoid(seg(8, D_MODEL)).astype(BF16)
    store_pairs(q_ref, seg(0) * (SB_HEAD_DIM ** -0.5 * LOG2_E))
    sgv_ref[0] = _sigmoid(seg(10, D_MODEL)).astype(BF16)
    store_pairs(k_ref, seg(1))
    store_pairs(sa_ref, _silu(seg(3)))
    store_pairs(v_ref, seg(2))
    gate_b = seg(5)
    gz_ref[0] = (gate_b * _silu(seg(7))).astype(BF16)
    u = seg(4)
    cu_ref[0] = (seg(6) * u).astype(BF16)


def _in_proj(x, w_in):
    b, s, d = x.shape
    tm = IN_ROW_TILE
    grid = (b, s // tm)
    pair_shape = jax.ShapeDtypeStruct((b, HEAD_PAIRS, s, LANES), BF16)
    half_shape = jax.ShapeDtypeStruct((b, s, SB_WIDTH), BF16)
    full_shape = jax.ShapeDtypeStruct((b, s, D_MODEL), BF16)
    pair_spec = pl.BlockSpec((1, HEAD_PAIRS, tm, LANES), lambda bi, ti: (bi, 0, ti, 0))
    half_spec = pl.BlockSpec((1, tm, SB_WIDTH), lambda bi, ti: (bi, ti, 0))
    full_spec = pl.BlockSpec((1, tm, D_MODEL), lambda bi, ti: (bi, ti, 0))
    return pl.pallas_call(
        _in_proj_kernel,
        grid=grid,
        in_specs=[
            pl.BlockSpec((1, tm, d), lambda bi, ti: (bi, ti, 0)),
            pl.BlockSpec(memory_space=pl.ANY),
        ],
        out_specs=[pair_spec] * 4 + [half_spec] * 2 + [full_spec] * 2,
        out_shape=[pair_shape] * 4 + [half_shape] * 2 + [full_shape] * 2,
        scratch_shapes=[
            pltpu.VMEM(w_in.shape, BF16),
            pltpu.VMEM((2, w_in.shape[0], W_CHUNK), F32),
            pltpu.SemaphoreType.DMA((2,)),
        ],
        compiler_params=pltpu.CompilerParams(
            dimension_semantics=("arbitrary", "arbitrary"), vmem_limit_bytes=VMEM_LIMIT_BYTES),
        name="in_proj",
    )(x, w_in)


def _softplus2(z):
    return jnp.maximum(z, 0.0) + jnp.log2(1.0 + jnp.exp2(jnp.minimum(z, -z)))


def _attn_kernel(q_ref, k_ref, v_ref, sa_ref, uu_ref, o_ref, qs_ref, acc_ref, ls_ref, flag_ref):
    s_len = q_ref.shape[2]
    nb = s_len // BLOCK

    def first_head(rows):
        return lax.broadcasted_iota(jnp.int32, (rows, LANES), 1) < SB_HEAD_DIM

    def offset(index, size):
        if isinstance(index, int):
            return index * size
        return pl.multiple_of(index * size, size)

    def stack_heads(q2):
        zero = jnp.zeros_like(q2)
        keep = first_head(q2.shape[0])
        return jnp.concatenate([jnp.where(keep, q2, zero), jnp.where(keep, zero, q2)], axis=0)

    def store_gated(rows, acc):
        n = acc.shape[0] // 2
        o = jnp.where(first_head(n), acc[:n], acc[n:])
        o_ref[0, 0, rows, :] = (o * sa_ref[0, 0, rows, :].astype(F32)).astype(o_ref.dtype)

    back_units = (WINDOW_KEYS - UNIT) // UNIT
    unit_row = lax.broadcasted_iota(jnp.int32, (2 * UNIT, BLOCK), 0) & (UNIT - 1)
    key_minus_row = lax.broadcasted_iota(jnp.int32, (2 * UNIT, BLOCK), 1) - unit_row

    def window_start(u):
        if isinstance(u, int):
            return max(u - back_units, 0)
        return u - back_units

    def key_bounds(u):
        first = window_start(u) if isinstance(u, int) else None
        bounds = []
        for b in range(WINDOW_BLOCKS):
            lead = (back_units if first is None else u - first) * UNIT - b * BLOCK
            bounds.append(None if lead >= BLOCK else lead)
        return bounds

    def scores(u):
        q2 = q_ref[0, 0, pl.ds(offset(u, UNIT), UNIT), :]
        kw = k_ref[0, 0, pl.ds(offset(window_start(u), UNIT), WINDOW_KEYS), :]
        return lax.dot_general(stack_heads(q2), kw, (((1,), (1,)), ((), ())),
                               preferred_element_type=F32)

    def split_logs(z, bounds):
        sp, log_hit = [], []
        for b, bound in enumerate(bounds):
            zb = z[:, b * BLOCK:(b + 1) * BLOCK]
            if bound is not None:
                zb = jnp.where(key_minus_row < bound, zb, MASKED_SCORE)
            softplus = _softplus2(zb)
            sp.append(softplus.astype(BF16))
            log_hit.append(zb - softplus)
        return sp, log_hit

    def later_sums(sp):
        older = jnp.dot(jnp.concatenate(sp, axis=1), uu_ref[...], preferred_element_type=F32)
        newer = jnp.dot(sp[1], uu_ref[:BLOCK, :BLOCK], preferred_element_type=F32)
        return older, newer

    def weights_of(sums, log_hit):
        older, newer = sums
        weights = [jnp.exp2(log_hit[0] + older[:, :BLOCK]).astype(BF16),
                   jnp.exp2(log_hit[1] + newer).astype(BF16)]
        return jnp.concatenate(weights, axis=1), older[:, BLOCK:]

    def finish(u, slot, weights, log_surv):
        vw = v_ref[0, 0, pl.ds(offset(window_start(u), UNIT), WINDOW_KEYS), :]
        acc = jnp.dot(weights, vw, preferred_element_type=F32)
        store_gated(pl.ds(offset(u, UNIT), UNIT), acc)
        if isinstance(u, int) and window_start(u) == 0:
            flag = jnp.full((SUBLANES, BLOCK), -LARGE, F32)
        else:
            flag = jnp.max(log_surv.reshape(2 * UNIT // SUBLANES, SUBLANES, BLOCK), axis=0)
        block, half = slot
        flag_ref[block, half * SUBLANES:(half + 1) * SUBLANES, :] = flag

    def window_group(g):
        units = [g * WINDOW_GROUP + n for n in range(WINDOW_GROUP)]
        slots = [(g * (WINDOW_GROUP // UNITS_PER_BLOCK) + n // UNITS_PER_BLOCK, n % UNITS_PER_BLOCK)
                 for n in range(WINDOW_GROUP)]
        zs = [scores(u) for u in units]
        logs = [split_logs(z, key_bounds(u)) for u, z in zip(units, zs)]
        sums = [later_sums(sp) for sp, _ in logs]
        probs = [weights_of(s, log_hit) for s, (_, log_hit) in zip(sums, logs)]
        for u, slot, (w, log_surv) in zip(units, slots, probs):
            finish(u, slot, w, log_surv)

    def traced_group(g, carry):
        window_group(g)
        return carry

    window_group(0)
    lax.fori_loop(1, s_len // (UNIT * WINDOW_GROUP), traced_group, 0)

    block_row = lax.broadcasted_iota(jnp.int32, (2 * BLOCK, BLOCK), 0) & (BLOCK - 1)
    strict = lax.broadcasted_iota(jnp.int32, (2 * BLOCK, BLOCK), 1) < block_row

    def tile(j, masked):
        k2 = k_ref[0, 0, pl.ds(offset(j, BLOCK), BLOCK), :]
        v2 = v_ref[0, 0, pl.ds(offset(j, BLOCK), BLOCK), :]
        z = lax.dot_general(qs_ref[...], k2, (((1,), (1,)), ((), ())),
                            preferred_element_type=F32)
        if masked:
            z = jnp.where(strict, z, MASKED_SCORE)
        softplus = _softplus2(z)
        sums = jnp.dot(softplus.astype(BF16), uu_ref[:BLOCK, :], preferred_element_type=F32)
        log_surv = ls_ref[...]
        a = jnp.exp2(z - softplus + sums[:, :BLOCK] + log_surv)
        acc_ref[...] += jnp.dot(a.astype(BF16), v2, preferred_element_type=F32)
        ls_ref[...] = log_surv + sums[:, BLOCK:]

    def general_block(i):
        rows = pl.ds(offset(i, BLOCK), BLOCK)
        qs_ref[...] = stack_heads(q_ref[0, 0, rows, :])
        acc_ref[...] = jnp.zeros_like(acc_ref)
        ls_ref[...] = jnp.zeros_like(ls_ref)
        tile(i, True)

        def more(c):
            j, worst = c
            return jnp.logical_and(j >= 0, worst > LOG2_SURV_CUTOFF)

        def step(c):
            j, _ = c
            tile(j, False)
            return j - 1, jnp.max(ls_ref[...])

        lax.while_loop(more, step, (i - 1, jnp.max(ls_ref[...])))
        store_gated(rows, acc_ref[...])

    @pl.when(jnp.max(flag_ref[...]) > LOG2_SURV_CUTOFF)
    def _():
        def maybe_redo(i, carry):
            @pl.when(jnp.max(flag_ref[i]) > LOG2_SURV_CUTOFF)
            def _():
                general_block(i)
            return carry

        lax.fori_loop(0, nb, maybe_redo, 0)


def _cumsum_weights():
    key = np.arange(2 * BLOCK)[:, None]
    col = np.arange(2 * BLOCK)[None, :]
    return jnp.asarray(np.where((col >= BLOCK) | (key > col), -1.0, 0.0), BF16)


def _attention(q4, k4, v4, sa4):
    b, pairs, s, _ = q4.shape
    assert s % (UNIT * WINDOW_GROUP) == 0 and WINDOW_GROUP % UNITS_PER_BLOCK == 0
    assert WINDOW_GROUP * UNIT >= WINDOW_KEYS
    assert WINDOW_BLOCKS == 2
    seq_spec = pl.BlockSpec((1, 1, s, LANES), lambda bi, pi: (bi, pi, 0, 0))
    uu = _cumsum_weights()
    return pl.pallas_call(
        _attn_kernel,
        grid=(b, pairs),
        in_specs=[seq_spec, seq_spec, seq_spec, seq_spec,
                  pl.BlockSpec(uu.shape, lambda bi, pi: (0, 0))],
        out_specs=seq_spec,
        out_shape=jax.ShapeDtypeStruct(q4.shape, BF16),
        scratch_shapes=[
            pltpu.VMEM((2 * BLOCK, LANES), BF16),
            pltpu.VMEM((2 * BLOCK, LANES), F32),
            pltpu.VMEM((2 * BLOCK, BLOCK), F32),
            pltpu.VMEM((s // BLOCK, UNITS_PER_BLOCK * SUBLANES, BLOCK), F32),
        ],
        compiler_params=pltpu.CompilerParams(
            dimension_semantics=("parallel", "parallel"), vmem_limit_bytes=VMEM_LIMIT_BYTES),
        name="stickbreak_attn",
    )(q4, k4, v4, sa4, uu)


def _shift_rows(cur, prev_tail, shift):
    rolled = pltpu.roll(cur, shift, 0)
    tail = pltpu.roll(prev_tail, shift, 0)
    row = lax.broadcasted_iota(jnp.int32, tail.shape, 0)
    top = jnp.where(row < shift, tail, rolled[:SUBLANES])
    return jnp.concatenate([top, rolled[SUBLANES:]], axis=0)


def _out_kernel(oa_ref, cu_ref, gz_ref, sga_ref, sgv_ref, x_ref, cuh_ref,
                cw_ref, wpa_ref, wpc_ref, wo_ref, gain_ref, bias_ref, out_ref):
    ti = pl.program_id(1)
    rows = out_ref.shape[1] // OUT_SUBTILES
    cw = cw_ref[...]

    def branch_inputs(n):
        lo = n * rows
        o_attn = jnp.concatenate(
            [oa_ref[0, p, lo:lo + rows, :] for p in range(HEAD_PAIRS)], axis=1)
        cu = cu_ref[0, lo:lo + rows, :].astype(F32)
        if n == 0:
            prev = jnp.where(ti > 0, cuh_ref[0].astype(F32), 0.0)
        else:
            prev = cu_ref[0, lo - SUBLANES:lo, :].astype(F32)
        conv = cw[CONV_KERNEL - 1:CONV_KERNEL] * cu
        for tap in range(CONV_KERNEL - 1):
            conv = conv + cw[tap:tap + 1] * _shift_rows(cu, prev, CONV_KERNEL - 1 - tap)
        o_conv = gz_ref[0, lo:lo + rows, :].astype(F32) * conv
        return o_attn, o_conv.astype(BF16)

    def merge(n, y_attn, y_conv):
        lo = n * rows
        merged = (sga_ref[0, lo:lo + rows, :].astype(F32) * y_attn
                  + sgv_ref[0, lo:lo + rows, :].astype(F32) * y_conv)
        return merged.astype(BF16)

    def finish(n, sub):
        lo = n * rows
        h = DEEPNORM_ALPHA * x_ref[0, lo:lo + rows, :] + sub
        mu = jnp.mean(h, axis=-1, keepdims=True)
        cen = h - mu
        var = jnp.mean(cen * cen, axis=-1, keepdims=True)
        out_ref[0, lo:lo + rows, :] = cen * lax.rsqrt(var + LN_EPS) * gain_ref[...] + bias_ref[...]

    subtiles = range(OUT_SUBTILES)
    branches = [branch_inputs(n) for n in subtiles]
    wpa = wpa_ref[...].astype(BF16)
    wpc = wpc_ref[...].astype(BF16)
    wo = wo_ref[...].astype(BF16)
    projected = [(jnp.dot(a, wpa, preferred_element_type=F32),
                  jnp.dot(c, wpc, preferred_element_type=F32)) for a, c in branches]
    merged = [merge(n, ya, yc) for n, (ya, yc) in zip(subtiles, projected)]
    subs = [jnp.dot(m, wo, preferred_element_type=F32) for m in merged]
    for n, sub in zip(subtiles, subs):
        finish(n, sub)


def _out_stage(oa4, cu, gz, sga, sgv, x, conv_w, wpa, wpc, wo, gain, bias):
    b, s, d = x.shape
    tm = OUT_ROW_TILE
    grid = (b, s // tm)
    halo_blocks = tm // SUBLANES
    half_spec = pl.BlockSpec((1, tm, SB_WIDTH), lambda bi, ti: (bi, ti, 0))
    full_spec = pl.BlockSpec((1, tm, D_MODEL), lambda bi, ti: (bi, ti, 0))
    halo_spec = pl.BlockSpec(
        (1, SUBLANES, CONV_WIDTH), lambda bi, ti: (bi, jnp.maximum(ti * halo_blocks - 1, 0), 0))

    def whole(a):
        return pl.BlockSpec(a.shape, lambda bi, ti: (0,) * a.ndim)

    return pl.pallas_call(
        _out_kernel,
        grid=grid,
        in_specs=[
            pl.BlockSpec((1, HEAD_PAIRS, tm, LANES), lambda bi, ti: (bi, 0, ti, 0)),
            half_spec, half_spec,
            full_spec, full_spec, full_spec,
            halo_spec,
            whole(conv_w), whole(wpa), whole(wpc), whole(wo), whole(gain), whole(bias),
        ],
        out_specs=full_spec,
        out_shape=jax.ShapeDtypeStruct((b, s, d), x.dtype),
        compiler_params=pltpu.CompilerParams(
            dimension_semantics=("parallel", "parallel"), vmem_limit_bytes=VMEM_LIMIT_BYTES),
        name="out_stage",
    )(oa4, cu, gz, sga, sgv, x, cu, conv_w, wpa, wpc, wo, gain, bias)


@jax.jit
def kernel(x, w_in, conv_w, w_proj_attn, w_proj_conv, w_out, ln_gain, ln_bias):
    h = x
    for layer in range(DEPTH):
        q4, k4, v4, sa4, cu, gz, sga, sgv = _in_proj(h, w_in[layer])
        oa4 = _attention(q4, k4, v4, sa4)
        h = _out_stage(
            oa4, cu, gz, sga, sgv, h,
            conv_w[layer], w_proj_attn[layer], w_proj_conv[layer], w_out[layer],
            ln_gain[layer][None, :], ln_bias[layer][None, :])
    return h
```

```python
import jax
import jax.numpy as jnp
import numpy as np
from jax import lax
from jax.experimental import pallas as pl
from jax.experimental.pallas import tpu as pltpu

D_MODEL = 1024
SB_HEADS = 8
SB_HEAD_DIM = 64
SB_WIDTH = SB_HEADS * SB_HEAD_DIM
CONV_WIDTH = D_MODEL // 2
CONV_KERNEL = 3
DEPTH = 1
DEEPNORM_ALPHA = (2.0 * DEPTH) ** 0.25
LN_EPS = 1e-5

LANES = 128
SUBLANES = 8
HEAD_PAIRS = SB_WIDTH // LANES
BLOCK = 128
UNIT = 64
UNITS_PER_BLOCK = BLOCK // UNIT
WINDOW_BLOCKS = 2
WINDOW_KEYS = WINDOW_BLOCKS * BLOCK
WINDOW_GROUP = 64
IN_ROW_TILE = 1024
W_CHUNK = 1024
OUT_ROW_TILE = 1024
OUT_SUBTILES = 4
VMEM_LIMIT_BYTES = 56 * 1024 * 1024
LOG2_E = 1.4426950408889634
LOG2_SURV_CUTOFF = -104.0 * LOG2_E
MASKED_SCORE = -1e4
LARGE = 1e30

BF16 = jnp.bfloat16
F32 = jnp.float32


def _twice_sigmoid(x):
    return 1.0 + jnp.tanh(0.5 * x)


def _silu(x):
    half = 0.5 * x
    return half + half * jnp.tanh(half)


def _in_proj_kernel(x_ref, w_hbm, q_ref, k_ref, v_ref, sa_ref, cu_ref, gz_ref, sga_ref, sgv_ref,
                    w_ref, stage_ref, sem):
    n_chunks = w_hbm.shape[1] // W_CHUNK

    def chunk_copy(c):
        slot = c % 2
        return pltpu.make_async_copy(
            w_hbm.at[:, pl.ds(c * W_CHUNK, W_CHUNK)], stage_ref.at[slot], sem.at[slot])

    @pl.when(jnp.logical_and(pl.program_id(0) == 0, pl.program_id(1) == 0))
    def _():
        chunk_copy(0).start()
        for c in range(n_chunks):
            if c + 1 < n_chunks:
                chunk_copy(c + 1).start()
            chunk_copy(c).wait()
            w_ref[:, c * W_CHUNK:(c + 1) * W_CHUNK] = stage_ref[c % 2].astype(BF16)

    xb = x_ref[0].astype(BF16)

    def seg(index, width=SB_WIDTH):
        c0 = index * SB_WIDTH
        return jnp.dot(xb, w_ref[:, c0:c0 + width], preferred_element_type=F32)

    def store_pairs(ref, y):
        for p in range(HEAD_PAIRS):
            ref[0, p] = y[:, p * LANES:(p + 1) * LANES].astype(BF16)

    sga_ref[0] = _twice_sigmoid(seg(8, D_MODEL)).astype(BF16)
    store_pairs(q_ref, seg(0) * (SB_HEAD_DIM ** -0.5 * LOG2_E))
    sgv_ref[0] = _twice_sigmoid(seg(10, D_MODEL)).astype(BF16)
    store_pairs(k_ref, seg(1))
    store_pairs(sa_ref, _silu(seg(3)))
    store_pairs(v_ref, seg(2))
    gate_b = seg(5)
    gz_ref[0] = (gate_b * _silu(seg(7))).astype(BF16)
    u = seg(4)
    cu_ref[0] = (seg(6) * u).astype(BF16)


def _in_proj(x, w_in):
    b, s, d = x.shape
    tm = IN_ROW_TILE
    grid = (b, s // tm)
    pair_shape = jax.ShapeDtypeStruct((b, HEAD_PAIRS, s, LANES), BF16)
    half_shape = jax.ShapeDtypeStruct((b, s, SB_WIDTH), BF16)
    full_shape = jax.ShapeDtypeStruct((b, s, D_MODEL), BF16)
    pair_spec = pl.BlockSpec((1, HEAD_PAIRS, tm, LANES), lambda bi, ti: (bi, 0, ti, 0))
    half_spec = pl.BlockSpec((1, tm, SB_WIDTH), lambda bi, ti: (bi, ti, 0))
    full_spec = pl.BlockSpec((1, tm, D_MODEL), lambda bi, ti: (bi, ti, 0))
    return pl.pallas_call(
        _in_proj_kernel,
        grid=grid,
        in_specs=[
            pl.BlockSpec((1, tm, d), lambda bi, ti: (bi, ti, 0)),
            pl.BlockSpec(memory_space=pl.ANY),
        ],
        out_specs=[pair_spec] * 4 + [half_spec] * 2 + [full_spec] * 2,
        out_shape=[pair_shape] * 4 + [half_shape] * 2 + [full_shape] * 2,
        scratch_shapes=[
            pltpu.VMEM(w_in.shape, BF16),
            pltpu.VMEM((2, w_in.shape[0], W_CHUNK), F32),
            pltpu.SemaphoreType.DMA((2,)),
        ],
        compiler_params=pltpu.CompilerParams(
            dimension_semantics=("arbitrary", "arbitrary"), vmem_limit_bytes=VMEM_LIMIT_BYTES),
        name="in_proj",
    )(x, w_in)


def _softplus2(z):
    return jnp.maximum(z, 0.0) + jnp.log2(1.0 + jnp.exp2(jnp.minimum(z, -z)))


def _attn_kernel(q_ref, k_ref, v_ref, sa_ref, uu_ref, o_ref, qs_ref, acc_ref, ls_ref, flag_ref):
    s_len = q_ref.shape[2]
    nb = s_len // BLOCK

    def first_head(rows):
        return lax.broadcasted_iota(jnp.int32, (rows, LANES), 1) < SB_HEAD_DIM

    def offset(index, size):
        if isinstance(index, int):
            return index * size
        return pl.multiple_of(index * size, size)

    def stack_heads(q2):
        zero = jnp.zeros_like(q2)
        keep = first_head(q2.shape[0])
        return jnp.concatenate([jnp.where(keep, q2, zero), jnp.where(keep, zero, q2)], axis=0)

    def store_gated(rows, acc):
        n = acc.shape[0] // 2
        o = jnp.where(first_head(n), acc[:n], acc[n:])
        o_ref[0, 0, rows, :] = (o * sa_ref[0, 0, rows, :].astype(F32)).astype(o_ref.dtype)

    back_units = (WINDOW_KEYS - UNIT) // UNIT
    unit_row = lax.broadcasted_iota(jnp.int32, (2 * UNIT, BLOCK), 0) & (UNIT - 1)
    key_minus_row = lax.broadcasted_iota(jnp.int32, (2 * UNIT, BLOCK), 1) - unit_row

    def window_start(u):
        if isinstance(u, int):
            return max(u - back_units, 0)
        return u - back_units

    def key_bounds(u):
        first = window_start(u) if isinstance(u, int) else None
        bounds = []
        for b in range(WINDOW_BLOCKS):
            lead = (back_units if first is None else u - first) * UNIT - b * BLOCK
            bounds.append(None if lead >= BLOCK else lead)
        return bounds

    def scores(u):
        q2 = q_ref[0, 0, pl.ds(offset(u, UNIT), UNIT), :]
        kw = k_ref[0, 0, pl.ds(offset(window_start(u), UNIT), WINDOW_KEYS), :]
        return lax.dot_general(stack_heads(q2), kw, (((1,), (1,)), ((), ())),
                               preferred_element_type=F32)

    def split_logs(z, bounds):
        sp, log_hit = [], []
        for b, bound in enumerate(bounds):
            zb = z[:, b * BLOCK:(b + 1) * BLOCK]
            if bound is not None:
                zb = jnp.where(key_minus_row < bound, zb, MASKED_SCORE)
            softplus = _softplus2(zb)
            sp.append(softplus.astype(BF16))
            log_hit.append(zb - softplus)
        return sp, log_hit

    def later_sums(sp):
        older = jnp.dot(jnp.concatenate(sp, axis=1), uu_ref[...], preferred_element_type=F32)
        newer = jnp.dot(sp[1], uu_ref[:BLOCK, :BLOCK], preferred_element_type=F32)
        return older, newer

    def weights_of(sums, log_hit):
        older, newer = sums
        weights = [jnp.exp2(log_hit[0] + older[:, :BLOCK]).astype(BF16),
                   jnp.exp2(log_hit[1] + newer).astype(BF16)]
        return jnp.concatenate(weights, axis=1), older[:, BLOCK:]

    def finish(u, slot, weights, log_surv):
        vw = v_ref[0, 0, pl.ds(offset(window_start(u), UNIT), WINDOW_KEYS), :]
        acc = jnp.dot(weights, vw, preferred_element_type=F32)
        store_gated(pl.ds(offset(u, UNIT), UNIT), acc)
        if isinstance(u, int) and window_start(u) == 0:
            flag = jnp.full((SUBLANES, BLOCK), -LARGE, F32)
        else:
            flag = jnp.max(log_surv.reshape(2 * UNIT // SUBLANES, SUBLANES, BLOCK), axis=0)
        block, half = slot
        flag_ref[block, half * SUBLANES:(half + 1) * SUBLANES, :] = flag

    def window_group(g):
        units = [g * WINDOW_GROUP + n for n in range(WINDOW_GROUP)]
        slots = [(g * (WINDOW_GROUP // UNITS_PER_BLOCK) + n // UNITS_PER_BLOCK, n % UNITS_PER_BLOCK)
                 for n in range(WINDOW_GROUP)]
        zs = [scores(u) for u in units]
        logs = [split_logs(z, key_bounds(u)) for u, z in zip(units, zs)]
        sums = [later_sums(sp) for sp, _ in logs]
        probs = [weights_of(s, log_hit) for s, (_, log_hit) in zip(sums, logs)]
        for u, slot, (w, log_surv) in zip(units, slots, probs):
            finish(u, slot, w, log_surv)

    def traced_group(g, carry):
        window_group(g)
        return carry

    window_group(0)
    lax.fori_loop(1, s_len // (UNIT * WINDOW_GROUP), traced_group, 0)

    block_row = lax.broadcasted_iota(jnp.int32, (2 * BLOCK, BLOCK), 0) & (BLOCK - 1)
    strict = lax.broadcasted_iota(jnp.int32, (2 * BLOCK, BLOCK), 1) < block_row

    def tile(j, masked):
        k2 = k_ref[0, 0, pl.ds(offset(j, BLOCK), BLOCK), :]
        v2 = v_ref[0, 0, pl.ds(offset(j, BLOCK), BLOCK), :]
        z = lax.dot_general(qs_ref[...], k2, (((1,), (1,)), ((), ())),
                            preferred_element_type=F32)
        if masked:
            z = jnp.where(strict, z, MASKED_SCORE)
        softplus = _softplus2(z)
        sums = jnp.dot(softplus.astype(BF16), uu_ref[:BLOCK, :], preferred_element_type=F32)
        log_surv = ls_ref[...]
        a = jnp.exp2(z - softplus + sums[:, :BLOCK] + log_surv)
        acc_ref[...] += jnp.dot(a.astype(BF16), v2, preferred_element_type=F32)
        ls_ref[...] = log_surv + sums[:, BLOCK:]

    def general_block(i):
        rows = pl.ds(offset(i, BLOCK), BLOCK)
        qs_ref[...] = stack_heads(q_ref[0, 0, rows, :])
        acc_ref[...] = jnp.zeros_like(acc_ref)
        ls_ref[...] = jnp.zeros_like(ls_ref)
        tile(i, True)

        def more(c):
            j, worst = c
            return jnp.logical_and(j >= 0, worst > LOG2_SURV_CUTOFF)

        def step(c):
            j, _ = c
            tile(j, False)
            return j - 1, jnp.max(ls_ref[...])

        lax.while_loop(more, step, (i - 1, jnp.max(ls_ref[...])))
        store_gated(rows, acc_ref[...])

    @pl.when(jnp.max(flag_ref[...]) > LOG2_SURV_CUTOFF)
    def _():
        def maybe_redo(i, carry):
            @pl.when(jnp.max(flag_ref[i]) > LOG2_SURV_CUTOFF)
            def _():
                general_block(i)
            return carry

        lax.fori_loop(0, nb, maybe_redo, 0)


def _cumsum_weights():
    key = np.arange(2 * BLOCK)[:, None]
    col = np.arange(2 * BLOCK)[None, :]
    return jnp.asarray(np.where((col >= BLOCK) | (key > col), -1.0, 0.0), BF16)


def _attention(q4, k4, v4, sa4):
    b, pairs, s, _ = q4.shape
    assert s % (UNIT * WINDOW_GROUP) == 0 and WINDOW_GROUP % UNITS_PER_BLOCK == 0
    assert WINDOW_GROUP * UNIT >= WINDOW_KEYS
    assert WINDOW_BLOCKS == 2
    seq_spec = pl.BlockSpec((1, 1, s, LANES), lambda bi, pi: (bi, pi, 0, 0))
    uu = _cumsum_weights()
    return pl.pallas_call(
        _attn_kernel,
        grid=(b, pairs),
        in_specs=[seq_spec, seq_spec, seq_spec, seq_spec,
                  pl.BlockSpec(uu.shape, lambda bi, pi: (0, 0))],
        out_specs=seq_spec,
        out_shape=jax.ShapeDtypeStruct(q4.shape, BF16),
        scratch_shapes=[
            pltpu.VMEM((2 * BLOCK, LANES), BF16),
            pltpu.VMEM((2 * BLOCK, LANES), F32),
            pltpu.VMEM((2 * BLOCK, BLOCK), F32),
            pltpu.VMEM((s // BLOCK, UNITS_PER_BLOCK * SUBLANES, BLOCK), F32),
        ],
        compiler_params=pltpu.CompilerParams(
            dimension_semantics=("parallel", "parallel"), vmem_limit_bytes=VMEM_LIMIT_BYTES),
        name="stickbreak_attn",
    )(q4, k4, v4, sa4, uu)


def _shift_rows(cur, prev_tail, shift):
    rolled = pltpu.roll(cur, shift, 0)
    tail = pltpu.roll(prev_tail, shift, 0)
    row = lax.broadcasted_iota(jnp.int32, tail.shape, 0)
    top = jnp.where(row < shift, tail, rolled[:SUBLANES])
    return jnp.concatenate([top, rolled[SUBLANES:]], axis=0)


def _out_kernel(oa_ref, cu_ref, gz_ref, sga_ref, sgv_ref, x_ref, cuh_ref,
                cw_ref, wpa_ref, wpc_ref, wo_ref, gain_ref, bias_ref, out_ref):
    ti = pl.program_id(1)
    rows = out_ref.shape[1] // OUT_SUBTILES
    cw = cw_ref[...]

    def branch_inputs(n):
        lo = n * rows
        o_attn = jnp.concatenate(
            [oa_ref[0, p, lo:lo + rows, :] for p in range(HEAD_PAIRS)], axis=1)
        cu = cu_ref[0, lo:lo + rows, :].astype(F32)
        if n == 0:
            prev = jnp.where(ti > 0, cuh_ref[0].astype(F32), 0.0)
        else:
            prev = cu_ref[0, lo - SUBLANES:lo, :].astype(F32)
        conv = cw[CONV_KERNEL - 1:CONV_KERNEL] * cu
        for tap in range(CONV_KERNEL - 1):
            conv = conv + cw[tap:tap + 1] * _shift_rows(cu, prev, CONV_KERNEL - 1 - tap)
        o_conv = gz_ref[0, lo:lo + rows, :].astype(F32) * conv
        return o_attn, o_conv.astype(BF16)

    def merge(n, y_attn, y_conv):
        lo = n * rows
        merged = (sga_ref[0, lo:lo + rows, :].astype(F32) * y_attn
                  + sgv_ref[0, lo:lo + rows, :].astype(F32) * y_conv)
        return merged.astype(BF16)

    def finish(n, sub):
        lo = n * rows
        h = DEEPNORM_ALPHA * x_ref[0, lo:lo + rows, :] + sub
        mu = jnp.mean(h, axis=-1, keepdims=True)
        cen = h - mu
        var = jnp.mean(cen * cen, axis=-1, keepdims=True)
        out_ref[0, lo:lo + rows, :] = cen * lax.rsqrt(var + LN_EPS) * gain_ref[...] + bias_ref[...]

    subtiles = range(OUT_SUBTILES)
    branches = [branch_inputs(n) for n in subtiles]
    wpa = wpa_ref[...].astype(BF16)
    wpc = wpc_ref[...].astype(BF16)
    wo = (0.5 * wo_ref[...]).astype(BF16)
    projected = [(jnp.dot(a, wpa, preferred_element_type=F32),
                  jnp.dot(c, wpc, preferred_element_type=F32)) for a, c in branches]
    merged = [merge(n, ya, yc) for n, (ya, yc) in zip(subtiles, projected)]
    subs = [jnp.dot(m, wo, preferred_element_type=F32) for m in merged]
    for n, sub in zip(subtiles, subs):
        finish(n, sub)


def _out_stage(oa4, cu, gz, sga, sgv, x, conv_w, wpa, wpc, wo, gain, bias):
    b, s, d = x.shape
    tm = OUT_ROW_TILE
    grid = (b, s // tm)
    halo_blocks = tm // SUBLANES
    half_spec = pl.BlockSpec((1, tm, SB_WIDTH), lambda bi, ti: (bi, ti, 0))
    full_spec = pl.BlockSpec((1, tm, D_MODEL), lambda bi, ti: (bi, ti, 0))
    halo_spec = pl.BlockSpec(
        (1, SUBLANES, CONV_WIDTH), lambda bi, ti: (bi, jnp.maximum(ti * halo_blocks - 1, 0), 0))

    def whole(a):
        return pl.BlockSpec(a.shape, lambda bi, ti: (0,) * a.ndim)

    return pl.pallas_call(
        _out_kernel,
        grid=grid,
        in_specs=[
            pl.BlockSpec((1, HEAD_PAIRS, tm, LANES), lambda bi, ti: (bi, 0, ti, 0)),
            half_spec, half_spec,
            full_spec, full_spec, full_spec,
            halo_spec,
            whole(conv_w), whole(wpa), whole(wpc), whole(wo), whole(gain), whole(bias),
        ],
        out_specs=full_spec,
        out_shape=jax.ShapeDtypeStruct((b, s, d), x.dtype),
        compiler_params=pltpu.CompilerParams(
            dimension_semantics=("parallel", "parallel"), vmem_limit_bytes=VMEM_LIMIT_BYTES),
        name="out_stage",
    )(oa4, cu, gz, sga, sgv, x, cu, conv_w, wpa, wpc, wo, gain, bias)


@jax.jit
def kernel(x, w_in, conv_w, w_proj_attn, w_proj_conv, w_out, ln_gain, ln_bias):
    h = x
    for layer in range(DEPTH):
        q4, k4, v4, sa4, cu, gz, sga, sgv = _in_proj(h, w_in[layer])
        oa4 = _attention(q4, k4, v4, sa4)
        h = _out_stage(
            oa4, cu, gz, sga, sgv, h,
            conv_w[layer], w_proj_attn[layer], w_proj_conv[layer], w_out[layer],
            ln_gain[layer][None, :], ln_bias[layer][None, :])
    return h
```
